```python
import jax, jax.numpy as jnp
from jax import lax
import numpy as np

D_MODEL = 2048
BATCH = 1
SEQ = 8192
DEPTH = 4

CHUNK = 64
N_A_LAYERS = DEPTH // 2
N_B_LAYERS = DEPTH - N_A_LAYERS
RWKV_HEAD = 64
RWKV_HEADS = D_MODEL // RWKV_HEAD
DECAY_LORA = 96
AAA_LORA = 96
MV_LORA = 64
GATE_LORA = 256
N_MIX = 6
LNX_EPS = 1e-5 * RWKV_HEAD
FOX_HEAD = 128
FOX_HEADS = D_MODEL // FOX_HEAD
Q_BLOCK = 128
D_FF = 4 * D_MODEL
RMS_EPS = 1e-6

kernel_name = "rwkv7_fox_yoco_hybrid"


def rms_norm(x, gain):
    xf = x.astype(jnp.float32)
    y = xf * lax.rsqrt(jnp.mean(xf * xf, axis=-1, keepdims=True) + RMS_EPS)
    return (y * gain.astype(jnp.float32)).astype(x.dtype)


def squared_relu_mlp(h, w_up, w_down):
    return jnp.square(jax.nn.relu(h @ w_up)) @ w_down


def rwkv7_recurrence(r, decay, k, v, a, b):
    B, S, H, N = r.shape
    n_chunks = S // CHUNK

    def to_chunks(t):
        return t.reshape(B, n_chunks, CHUNK, H, N).transpose(1, 2, 0, 3, 4)

    def step(state, inp):
        r_t, w_t, k_t, v_t, a_t, b_t = inp
        sa = jnp.einsum('bhvk,bhk->bhv', state, a_t)
        state = (state * w_t[:, :, None, :]
                 + sa[..., None] * b_t[:, :, None, :]
                 + v_t[..., None] * k_t[:, :, None, :])
        y_t = jnp.einsum('bhvk,bhk->bhv', state, r_t)
        return state, y_t

    def chunk_step(state, chunk_inp):
        return lax.scan(step, state, chunk_inp)

    state0 = jnp.zeros((B, H, N, N), jnp.float32)
    inputs = (to_chunks(r), to_chunks(decay), to_chunks(k), to_chunks(v), to_chunks(a), to_chunks(b))
    _, y = lax.scan(chunk_step, state0, inputs)
    return y.transpose(2, 0, 1, 3, 4).reshape(B, S, H, N)


def rwkv7_time_mix(h, x_mix, w_rkv, w0, w1, w2, a0, a1, a2, g1, g2, k_k, k_a, r_k,
                   lnx_w, lnx_b, w_o, v_first, v_res):
    B, S, D = h.shape
    H, N = RWKV_HEADS, RWKV_HEAD
    f32 = jnp.float32
    prev = jnp.pad(h, ((0, 0), (1, 0), (0, 0)))[:, :S]
    xs = h[:, :, None, :] + (prev - h)[:, :, None, :] * x_mix
    rkv = jnp.einsum('bsnd,nde->bsne', xs[:, :, :3], w_rkv)
    r, k, v = rkv[:, :, 0], rkv[:, :, 1], rkv[:, :, 2]
    xv, xw, xa, xg = xs[:, :, 2], xs[:, :, 3], xs[:, :, 4], xs[:, :, 5]

    w_log = -jax.nn.softplus(-(w0 + jnp.tanh(xw @ w1) @ w2).astype(f32)) - 0.5
    decay = jnp.exp(-jnp.exp(w_log))
    if v_res is None:
        v_first = v
    else:
        v0, v1, v2 = v_res
        v = v + (v_first - v) * jax.nn.sigmoid(v0 + (xv @ v1) @ v2)
    a = jax.nn.sigmoid((a0 + (xa @ a1) @ a2).astype(f32))
    g = jax.nn.sigmoid(xg @ g1) @ g2

    heads = lambda t: t.astype(f32).reshape(B, S, H, N)
    kk = heads(k * k_k)
    kk = kk / jnp.maximum(jnp.sqrt(jnp.sum(kk * kk, axis=-1, keepdims=True)), 1e-12)
    k = k.astype(f32) * (1.0 + (a - 1.0) * k_a.astype(f32))
    rh, kh, vh, ah, dh = heads(r), heads(k), heads(v), heads(a), heads(decay)

    y = rwkv7_recurrence(rh, dh, kh, vh, -kk, kk * ah)
    mu = jnp.mean(y, axis=-1, keepdims=True)
    var = jnp.mean(jnp.square(y - mu), axis=-1, keepdims=True)
    y = ((y - mu) * lax.rsqrt(var + LNX_EPS)).reshape(B, S, D)
    y = y * lnx_w.astype(f32) + lnx_b.astype(f32)
    bonus = jnp.sum(rh * kh * r_k.astype(f32), axis=-1, keepdims=True) * vh
    y = y + bonus.reshape(B, S, D)
    out = (y * g.astype(f32)).astype(w_o.dtype) @ w_o
    return out.astype(h.dtype), v_first


def shared_kv_forget(x, kv_norm, w_kvf, b_f):
    B, S, D = x.shape
    proj = rms_norm(x, kv_norm) @ w_kvf
    k = proj[..., :D].reshape(B, S, FOX_HEADS, FOX_HEAD).transpose(0, 2, 1, 3)
    v = proj[..., D:2 * D].reshape(B, S, FOX_HEADS, FOX_HEAD).transpose(0, 2, 1, 3)
    log_f = jax.nn.log_sigmoid((proj[..., 2 * D:] + b_f).astype(jnp.float32))
    cum_log_f = jnp.cumsum(log_f.transpose(0, 2, 1), axis=-1)
    return k, v, cum_log_f


def forgetting_attention(h, w_q, w_o, k, v, cum_log_f):
    B, S, D = h.shape
    q = (h @ w_q).reshape(B, S, FOX_HEADS, FOX_HEAD).transpose(0, 2, 1, 3)
    scale = FOX_HEAD ** -0.5
    outs = []
    for blk in range(S // Q_BLOCK):
        q0, q1 = blk * Q_BLOCK, (blk + 1) * Q_BLOCK
        kb, vb = k[:, :, :q1], v[:, :, :q1]
        s = jnp.einsum('bhqd,bhkd->bhqk', q[:, :, q0:q1], kb).astype(jnp.float32) * scale
        s = s + cum_log_f[:, :, q0:q1, None] - cum_log_f[:, :, None, :q1]
        causal = jnp.arange(q0, q1)[:, None] >= jnp.arange(q1)[None, :]
        p = jax.nn.softmax(jnp.where(causal, s, -jnp.inf), axis=-1)
        outs.append(jnp.einsum('bhqk,bhkd->bhqd', p.astype(vb.dtype), vb))
    o = jnp.concatenate(outs, axis=2).transpose(0, 2, 1, 3).reshape(B, S, D)
    return o @ w_o


def setup_inputs(seed: int = 0) -> dict:
    key = jax.random.key(seed)
    ks = iter(jax.random.split(key, 40))
    D = D_MODEL
    nrm = lambda shape, scale: jax.random.normal(next(ks), shape, jnp.float32) * scale
    unif = lambda shape, lo, hi: jax.random.uniform(next(ks), shape, jnp.float32, lo, hi)
    na, nb = N_A_LAYERS, N_B_LAYERS
    return {
        "x": nrm((BATCH, SEQ, D), 1.0),
        "mix_norm": 1.0 + nrm((DEPTH, D), 0.02),
        "ffn_norm": 1.0 + nrm((DEPTH, D), 0.02),
        "final_norm": 1.0 + nrm((D,), 0.02),
        "rwkv_x_mix": unif((na, N_MIX, D), 0.0, 1.0),
        "rwkv_w_rkv": nrm((na, 3, D, D), D ** -0.5),
        "rwkv_w0": unif((na, D), -6.0, -1.0),
        "rwkv_w1": nrm((na, D, DECAY_LORA), D ** -0.5),
        "rwkv_w2": nrm((na, DECAY_LORA, D), 0.5 * DECAY_LORA ** -0.5),
        "rwkv_a0": nrm((na, D), 0.1),
        "rwkv_a1": nrm((na, D, AAA_LORA), D ** -0.5),
        "rwkv_a2": nrm((na, AAA_LORA, D), AAA_LORA ** -0.5),
        "rwkv_v0": nrm((na - 1, D), 0.1),
        "rwkv_v1": nrm((na - 1, D, MV_LORA), D ** -0.5),
        "rwkv_v2": nrm((na - 1, MV_LORA, D), MV_LORA ** -0.5),
        "rwkv_g1": nrm((na, D, GATE_LORA), D ** -0.5),
        "rwkv_g2": nrm((na, GATE_LORA, D), GATE_LORA ** -0.5),
        "rwkv_k_k": 0.85 + nrm((na, D), 0.02),
        "rwkv_k_a": 1.0 + nrm((na, D), 0.02),
        "rwkv_r_k": nrm((na, RWKV_HEADS, RWKV_HEAD), 0.1),
        "rwkv_lnx_w": 1.0 + nrm((na, D), 0.02),
        "rwkv_lnx_b": nrm((na, D), 0.02),
        "rwkv_w_o": nrm((na, D, D), D ** -0.5),
        "kv_norm": 1.0 + nrm((D,), 0.02),
        "w_kvf": nrm((D, 2 * D + FOX_HEADS), D ** -0.5),
        "b_f": nrm((FOX_HEADS,), 0.1),
        "fox_w_q": nrm((nb, D, D), D ** -0.5),
        "fox_w_o": nrm((nb, D, D), D ** -0.5),
        "mlp_w_up": nrm((DEPTH, D, D_FF), D ** -0.5),
        "mlp_w_down": nrm((DEPTH, D_FF, D), D_FF ** -0.5),
    }


def reference(x, mix_norm, ffn_norm, final_norm, rwkv_x_mix, rwkv_w_rkv, rwkv_w0, rwkv_w1,
              rwkv_w2, rwkv_a0, rwkv_a1, rwkv_a2, rwkv_v0, rwkv_v1, rwkv_v2, rwkv_g1, rwkv_g2,
              rwkv_k_k, rwkv_k_a, rwkv_r_k, rwkv_lnx_w, rwkv_lnx_b, rwkv_w_o, kv_norm, w_kvf,
              b_f, fox_w_q, fox_w_o, mlp_w_up, mlp_w_down):
    v_first = None
    k_sh = v_sh = c_sh = None
    for layer in range(DEPTH):
        if layer < N_A_LAYERS:
            i = layer
            v_res = None if i == 0 else (rwkv_v0[i - 1], rwkv_v1[i - 1], rwkv_v2[i - 1])
            h = rms_norm(x, mix_norm[layer])
            mixed, v_first = rwkv7_time_mix(
                h, rwkv_x_mix[i], rwkv_w_rkv[i], rwkv_w0[i], rwkv_w1[i], rwkv_w2[i],
                rwkv_a0[i], rwkv_a1[i], rwkv_a2[i], rwkv_g1[i], rwkv_g2[i], rwkv_k_k[i],
                rwkv_k_a[i], rwkv_r_k[i], rwkv_lnx_w[i], rwkv_lnx_b[i], rwkv_w_o[i],
                v_first, v_res)
        else:
            if layer == N_A_LAYERS:
                k_sh, v_sh, c_sh = shared_kv_forget(x, kv_norm, w_kvf, b_f)
            j = layer - N_A_LAYERS
            h = rms_norm(x, mix_norm[layer])
            mixed = forgetting_attention(h, fox_w_q[j], fox_w_o[j], k_sh, v_sh, c_sh)
        x = x + mixed
        x = x + squared_relu_mlp(rms_norm(x, ffn_norm[layer]), mlp_w_up[layer], mlp_w_down[layer])
    return rms_norm(x, final_norm)
```

```python
import functools

import jax
import jax.numpy as jnp
from jax import lax
from jax.experimental import pallas as pl
from jax.experimental.pallas import tpu as pltpu

F32 = jnp.float32
BF16 = jnp.bfloat16

RWKV_HEAD = 64
FOX_HEAD = 128
RMS_EPS = 1e-6
LNX_EPS = 1e-5 * RWKV_HEAD
V7X_LANES = 128
REC_CHUNK = 64
V7X_VMEM_LIMIT = 56 * 1024 * 1024
MASK_VALUE = -1e30


def _cparams(*sem):
    return pltpu.CompilerParams(dimension_semantics=sem, vmem_limit_bytes=V7X_VMEM_LIMIT)


def _sigmoid(z):
    return 1.0 / (1.0 + jnp.exp(-z))


def _log_sigmoid(z):
    return jnp.minimum(z, 0.0) - jnp.log1p(jnp.exp(-jnp.abs(z)))


def _rms(x, gain):
    return x * lax.rsqrt(jnp.mean(x * x, axis=-1, keepdims=True) + RMS_EPS) * gain


def _head_sum_matrix(n, head):
    r = lax.broadcasted_iota(jnp.int32, (n, n), 0) // head
    c = lax.broadcasted_iota(jnp.int32, (n, n), 1) // head
    return (r == c).astype(BF16)


def _group_sum(x, ones_bd):
    hi = x.astype(BF16)
    lo = (x - hi.astype(F32)).astype(BF16)
    return (jnp.dot(hi, ones_bd, preferred_element_type=F32)
            + jnp.dot(lo, ones_bd, preferred_element_type=F32))


def _rms_kernel(x_ref, g_ref, o_ref):
    x = x_ref[...]
    y = x * lax.rsqrt(jnp.mean(x * x, axis=-1, keepdims=True) + RMS_EPS)
    for n in range(o_ref.shape[0]):
        o_ref[n] = (y * g_ref[n:n + 1, :]).astype(o_ref.dtype)


def rms_norm_multi(x, gains, out_dtype, tm=512):
    S, D = x.shape
    G = gains.shape[0]
    tm = min(tm, S)
    return pl.pallas_call(
        _rms_kernel,
        grid=(S // tm,),
        in_specs=[pl.BlockSpec((tm, D), lambda i: (i, 0)),
                  pl.BlockSpec((G, D), lambda i: (0, 0))],
        out_specs=pl.BlockSpec((G, tm, D), lambda i: (0, i, 0)),
        out_shape=jax.ShapeDtypeStruct((G, S, D), out_dtype),
        compiler_params=_cparams("parallel"),
        name="rms_norm",
    )(x, gains)


def _mix_kernel(x_ref, halo_ref, g_ref, mix_ref, o_ref):
    i = pl.program_id(0)
    gain = g_ref[...]
    h = _rms(x_ref[...], gain)
    hh = _rms(halo_ref[...], gain)
    last = jnp.where(i == 0, 0.0, hh[7:8, :])
    row = lax.broadcasted_iota(jnp.int32, h.shape, 0)
    prev = jnp.where(row == 0, last, pltpu.roll(h, 1, 0))
    d = prev - h
    for n in range(o_ref.shape[0]):
        o_ref[n] = (h + d * mix_ref[n:n + 1, :]).astype(o_ref.dtype)


def rwkv_mix(x, gain, x_mix, tm=256):
    S, D = x.shape
    n_mix = x_mix.shape[0]
    tm = min(tm, S)
    hb = tm // 8
    return pl.pallas_call(
        _mix_kernel,
        grid=(S // tm,),
        in_specs=[pl.BlockSpec((tm, D), lambda i: (i, 0)),
                  pl.BlockSpec((8, D), lambda i: (jnp.maximum(i * hb - 1, 0), 0)),
                  pl.BlockSpec((1, D), lambda i: (0, 0)),
                  pl.BlockSpec((n_mix, D), lambda i: (0, 0))],
        out_specs=pl.BlockSpec((n_mix, tm, D), lambda i: (0, i, 0)),
        out_shape=jax.ShapeDtypeStruct((n_mix, S, D), BF16),
        compiler_params=_cparams("parallel"),
        name="rwkv_mix",
    )(x, x, gain, x_mix)


def _mm_kernel(x_ref, w_ref, *rest, act, scale, has_bias, has_res):
    o_ref = rest[-1]
    acc = jnp.dot(x_ref[...], w_ref[...], preferred_element_type=F32)
    if scale != 1.0:
        acc = acc * scale
    if has_bias:
        acc = acc + rest[0][...]
    if act is not None:
        acc = act(acc)
    if has_res:
        acc = acc + rest[-2][...]
    o_ref[...] = acc.astype(o_ref.dtype)


def matmul(x, w, *, x_idx=None, act=None, scale=1.0, bias=None, res=None, out_dtype=F32,
           tm=1024, tn=1024, name="matmul"):
    batched = w.ndim == 3
    M, K = x.shape[-2:]
    N = w.shape[-1]
    tm, tn = min(tm, M), min(tn, N)
    kern = functools.partial(_mm_kernel, act=act, scale=scale, has_bias=bias is not None,
                             has_res=res is not None)
    if batched:
        nb = w.shape[0]
        grid = (nb, M // tm, N // tn)
        x_spec = pl.BlockSpec((None, tm, K), lambda b, i, j: (b, i, 0))
        w_spec = pl.BlockSpec((None, K, tn), lambda b, i, j: (b, 0, j))
        o_spec = pl.BlockSpec((None, tm, tn), lambda b, i, j: (b, i, j))
        out_shape = jax.ShapeDtypeStruct((nb, M, N), out_dtype)
        sem = ("parallel", "parallel", "parallel")
    else:
        grid = (M // tm, N // tn)
        if x_idx is None:
            x_spec = pl.BlockSpec((tm, K), lambda i, j: (i, 0))
        else:
            x_spec = pl.BlockSpec((None, tm, K), lambda i, j: (x_idx, i, 0))
        w_spec = pl.BlockSpec((K, tn), lambda i, j: (0, j))
        o_spec = pl.BlockSpec((tm, tn), lambda i, j: (i, j))
        out_shape = jax.ShapeDtypeStruct((M, N), out_dtype)
        sem = ("parallel", "parallel")
    in_specs = [x_spec, w_spec]
    args = [x, w]
    if bias is not None:
        assert not batched
        in_specs.append(pl.BlockSpec((1, tn), lambda i, j: (0, j)))
        args.append(bias)
    if res is not None:
        assert not batched
        in_specs.append(o_spec)
        args.append(res)
    return pl.pallas_call(
        kern, grid=grid, in_specs=in_specs, out_specs=o_spec, out_shape=out_shape,
        compiler_params=_cparams(*sem), name=name,
    )(*args)


def _pre_kernel(*refs, has_vres):
    if has_vres:
        (k_ref, v_ref, hw_ref, ha_ref, hv_ref, hg_ref, w2_ref, a2_ref, v2_ref, g2_ref,
         vec_ref, vf_ref, lw_ref, ko_ref, a_ref, b_ref, g_ref, vo_ref) = refs
    else:
        (k_ref, hw_ref, ha_ref, hg_ref, w2_ref, a2_ref, g2_ref,
         vec_ref, lw_ref, ko_ref, a_ref, b_ref, g_ref) = refs
    w0, a0, v0, k_k, k_a = (vec_ref[n:n + 1, :] for n in range(5))
    k = k_ref[...]
    D = k.shape[-1]

    z_w = w0 + jnp.dot(hw_ref[...], w2_ref[...], preferred_element_type=F32)
    lw_ref[...] = -jnp.exp(_log_sigmoid(z_w) - 0.5)
    a_sig = _sigmoid(a0 + jnp.dot(ha_ref[...], a2_ref[...], preferred_element_type=F32))
    g_ref[...] = jnp.dot(hg_ref[...], g2_ref[...], preferred_element_type=F32)
    if has_vres:
        v = v_ref[...]
        mix = _sigmoid(v0 + jnp.dot(hv_ref[...], v2_ref[...], preferred_element_type=F32))
        vo_ref[...] = v + (vf_ref[...] - v) * mix

    ko_ref[...] = k * (1.0 + (a_sig - 1.0) * k_a)
    kk = k * k_k
    ones_bd = _head_sum_matrix(V7X_LANES, RWKV_HEAD)
    for s in range(D // V7X_LANES):
        sl = slice(s * V7X_LANES, (s + 1) * V7X_LANES)
        kks = kk[:, sl]
        nrm = jnp.maximum(jnp.sqrt(_group_sum(kks * kks, ones_bd)), 1e-12)
        kkn = kks / nrm
        a_ref[:, sl] = -kkn
        b_ref[:, sl] = kkn * a_sig[:, sl]


def rwkv_pre(rkv, hw, ha, hv, hg, w2, a2, v2, g2, vecs, v_first, tm=128):
    _, S, D = rkv.shape
    tm = min(tm, S)
    has_vres = v_first is not None
    row = lambda i: (i, 0)
    full = lambda i: (0, 0)
    act = pl.BlockSpec((tm, D), row)
    rkv_spec = lambda n: pl.BlockSpec((None, tm, D), lambda i: (n, i, 0))
    lora = lambda a: pl.BlockSpec((tm, a.shape[1]), row)
    wspec = lambda a: pl.BlockSpec(a.shape, full)
    if has_vres:
        in_specs = [rkv_spec(1), rkv_spec(2), lora(hw), lora(ha), lora(hv), lora(hg),
                    wspec(w2), wspec(a2), wspec(v2), wspec(g2), wspec(vecs), rkv_spec(2)]
        args = [rkv, rkv, hw, ha, hv, hg, w2, a2, v2, g2, vecs, v_first]
        n_out = 6
    else:
        in_specs = [rkv_spec(1), lora(hw), lora(ha), lora(hg),
                    wspec(w2), wspec(a2), wspec(g2), wspec(vecs)]
        args = [rkv, hw, ha, hg, w2, a2, g2, vecs]
        n_out = 5
    return pl.pallas_call(
        functools.partial(_pre_kernel, has_vres=has_vres),
        grid=(S // tm,),
        in_specs=in_specs,
        out_specs=[act] * n_out,
        out_shape=[jax.ShapeDtypeStruct((S, D), F32)] * n_out,
        compiler_params=_cparams("parallel"),
        name="rwkv_pre",
    )(*args)


def _rec_kernel(r_ref, lw_ref, k_ref, v_ref, a_ref, b_ref, g_ref, lnw_ref, lnb_ref, rk_ref,
                o_ref, st_ref):
    T = REC_CHUNK
    W = V7X_LANES
    D = r_ref.shape[-1]

    @pl.when(pl.program_id(0) == 0)
    def _():
        st_ref[...] = jnp.zeros_like(st_ref)

    tri = (lax.broadcasted_iota(jnp.int32, (T, T), 1)
           <= lax.broadcasted_iota(jnp.int32, (T, T), 0)).astype(F32)
    head0 = lax.broadcasted_iota(jnp.int32, (T, W), 1) < RWKV_HEAD
    rr = lax.broadcasted_iota(jnp.int32, (2 * T, 2 * T), 0) & (T - 1)
    cc = lax.broadcasted_iota(jnp.int32, (2 * T, 2 * T), 1) & (T - 1)
    strict = cc < rr
    incl = cc <= rr
    ones_bd = _head_sum_matrix(W, RWKV_HEAD)

    def stack(x):
        return jnp.concatenate([jnp.where(head0, x, 0.0), jnp.where(head0, 0.0, x)], axis=0)

    def mm(x, y):
        return jnp.dot(x.astype(BF16), y.astype(BF16), preferred_element_type=F32)

    def mm_nt(x, y):
        return lax.dot_general(x.astype(BF16), y.astype(BF16), (((1,), (1,)), ((), ())),
                               preferred_element_type=F32)

    def mm_tn(x, y):
        return lax.dot_general(x.astype(BF16), y.astype(BF16), (((0,), (0,)), ((), ())),
                               preferred_element_type=F32)

    for p in range(D // W):
        sl = slice(p * W, (p + 1) * W)
        lw = lw_ref[:, sl]
        r, k, v = r_ref[:, sl], k_ref[:, sl], v_ref[:, sl]
        a, b = a_ref[:, sl], b_ref[:, sl]
        L = jnp.dot(tri, lw, precision=lax.Precision.HIGHEST, preferred_element_type=F32)
        e_neg = jnp.exp(-L)
        l_end = L[T - 1:T, :]
        to_end = jnp.exp(l_end - L)
        a_s = stack(a * jnp.exp(L - lw))
        r_s = stack(r * jnp.exp(L))
        b_s = stack(b * e_neg)
        k_s = stack(k * e_neg)
        v_s = stack(v)
        bd_s = stack(b * to_end)
        kd_s = stack(k * to_end)

        gram = mm_nt(jnp.concatenate([a_s, r_s], axis=0), jnp.concatenate([b_s, k_s], axis=0))
        a_ab = jnp.where(strict, gram[:2 * T, :2 * T], 0.0)
        a_ak = jnp.where(strict, gram[:2 * T, 2 * T:], 0.0)
        a_rb = jnp.where(incl, gram[2 * T:, :2 * T], 0.0)
        a_rk = jnp.where(incl, gram[2 * T:, 2 * T:], 0.0)
        av = mm(jnp.concatenate([a_ak, a_rk], axis=0), v_s)

        x = jnp.concatenate([a_s, av[:2 * T]], axis=1)
        n_pow = a_ab
        x = x + mm(n_pow, x)
        for _ in range(5):
            n_pow = mm(n_pow, n_pow)
            x = x + mm(n_pow, x)

        s0 = st_ref[p]
        wr = mm_nt(jnp.concatenate([x[:, :W], r_s], axis=0), s0)
        u_s = wr[:2 * T] + x[:, W:]
        y_s = wr[2 * T:] + mm(a_rb, u_s) + av[2 * T:]
        st_ref[p] = (s0 * jnp.exp(l_end)
                     + mm_tn(jnp.concatenate([u_s, v_s], axis=0),
                             jnp.concatenate([bd_s, kd_s], axis=0)))

        y = y_s[:T] + y_s[T:]
        inv_n = 1.0 / RWKV_HEAD
        mu = _group_sum(y, ones_bd) * inv_n
        d = y - mu
        var = _group_sum(d * d, ones_bd) * inv_n
        yn = d * lax.rsqrt(var + LNX_EPS) * lnw_ref[:, sl] + lnb_ref[:, sl]
        bonus = _group_sum(r * k * rk_ref[:, sl], ones_bd) * v
        o_ref[:, sl] = ((yn + bonus) * g_ref[:, sl]).astype(o_ref.dtype)


def rwkv_recurrence(r_arr, r_idx, lw, k, v_arr, v_idx, a, b, g, lnw, lnb, rk):
    S, D = lw.shape
    T = REC_CHUNK
    act = pl.BlockSpec((T, D), lambda c: (c, 0))
    vec = pl.BlockSpec((1, D), lambda c: (0, 0))

    def maybe_stacked(idx):
        if idx is None:
            return act
        return pl.BlockSpec((None, T, D), lambda c: (idx, c, 0))

    return pl.pallas_call(
        _rec_kernel,
        grid=(S // T,),
        in_specs=[maybe_stacked(r_idx), act, act, maybe_stacked(v_idx), act, act, act,
                  vec, vec, vec],
        out_specs=act,
        out_shape=jax.ShapeDtypeStruct((S, D), BF16),
        scratch_shapes=[pltpu.VMEM((D // V7X_LANES, V7X_LANES, V7X_LANES), F32)],
        compiler_params=_cparams("arbitrary"),
        name="rwkv_recurrence",
    )(r_arr, lw, k, v_arr, a, b, g, lnw, lnb, rk)


def _mlp_kernel(x_ref, g_ref, wu_ref, wd_ref, o_ref, xn_ref):
    @pl.when(pl.program_id(1) == 0)
    def _():
        x = x_ref[...]
        xn_ref[...] = _rms(x, g_ref[...]).astype(BF16)
        o_ref[...] = x

    h = jnp.dot(xn_ref[...], wu_ref[...], preferred_element_type=F32)
    h = jnp.square(jnp.maximum(h, 0.0)).astype(BF16)
    o_ref[...] += jnp.dot(h, wd_ref[...], preferred_element_type=F32)


def mlp_block(x, gain, w_up, w_down, tm=512, tf=1024):
    S, D = x.shape
    F = w_up.shape[1]
    tm, tf = min(tm, S), min(tf, F)
    return pl.pallas_call(
        _mlp_kernel,
        grid=(S // tm, F // tf),
        in_specs=[pl.BlockSpec((tm, D), lambda i, f: (i, 0)),
                  pl.BlockSpec((1, D), lambda i, f: (0, 0)),
                  pl.BlockSpec((D, tf), lambda i, f: (0, f)),
                  pl.BlockSpec((tf, D), lambda i, f: (f, 0))],
        out_specs=pl.BlockSpec((tm, D), lambda i, f: (i, 0)),
        out_shape=jax.ShapeDtypeStruct((S, D), F32),
        scratch_shapes=[pltpu.VMEM((tm, D), BF16)],
        compiler_params=_cparams("parallel", "arbitrary"),
        name="mlp",
    )(x, gain, w_up, w_down)


def _cumsum_kernel(x_ref, o_ref, carry_ref):
    @pl.when(pl.program_id(0) == 0)
    def _():
        carry_ref[...] = jnp.zeros_like(carry_ref)

    n = x_ref.shape[0]
    tri = (lax.broadcasted_iota(jnp.int32, (n, n), 1)
           <= lax.broadcasted_iota(jnp.int32, (n, n), 0)).astype(F32)
    c = jnp.dot(tri, x_ref[...], precision=lax.Precision.HIGHEST,
                preferred_element_type=F32) + carry_ref[0:1, :]
    o_ref[...] = c
    carry_ref[...] = jnp.broadcast_to(c[n - 1:n, :], carry_ref.shape)


def cumsum_rows(x, tc=256):
    S, W = x.shape
    tc = min(tc, S)
    return pl.pallas_call(
        _cumsum_kernel,
        grid=(S // tc,),
        in_specs=[pl.BlockSpec((tc, W), lambda i: (i, 0))],
        out_specs=pl.BlockSpec((tc, W), lambda i: (i, 0)),
        out_shape=jax.ShapeDtypeStruct((S, W), F32),
        scratch_shapes=[pltpu.VMEM((8, W), F32)],
        compiler_params=_cparams("arbitrary"),
        name="forget_cumsum",
    )(x)


def _fox_kernel(qt_ref, kt_ref, q_ref, k_ref, v_ref, cq_ref, ck_ref, o_ref,
                m_ref, l_ref, acc_ref, cqh_ref, *, tq, tk):
    h = pl.program_id(0)
    t = pl.program_id(1)
    qi = qt_ref[t]
    kj = kt_ref[t]
    last_kj = (qi + 1) * (tq // tk) - 1

    @pl.when(kj == 0)
    def _():
        m_ref[...] = jnp.full_like(m_ref, MASK_VALUE)
        l_ref[...] = jnp.zeros_like(l_ref)
        acc_ref[...] = jnp.zeros_like(acc_ref)
        lane = lax.broadcasted_iota(jnp.int32, cq_ref.shape, 1)
        cqh = jnp.sum(jnp.where(lane == h, cq_ref[...], 0.0), axis=1, keepdims=True)
        cqh_ref[...] = jnp.broadcast_to(cqh, cqh_ref.shape)

    def step(masked):
        s = lax.dot_general(q_ref[...], k_ref[...], (((1,), (1,)), ((), ())),
                            preferred_element_type=F32)
        s = s + (cqh_ref[:, 0:1] - ck_ref[...])
        if masked:
            row = qi * tq + lax.broadcasted_iota(jnp.int32, (tq, tk), 0)
            col = kj * tk + lax.broadcasted_iota(jnp.int32, (tq, tk), 1)
            s = jnp.where(row >= col, s, MASK_VALUE)
        m_prev = m_ref[:, 0:1]
        m_new = jnp.maximum(m_prev, jnp.max(s, axis=1, keepdims=True))
        alpha = jnp.exp(m_prev - m_new)
        p = jnp.exp(s - m_new)
        l_ref[...] = alpha * l_ref[...] + jnp.sum(p, axis=1, keepdims=True)
        acc_ref[...] = alpha * acc_ref[...] + jnp.dot(p.astype(BF16), v_ref[...],
                                                      preferred_element_type=F32)
        m_ref[...] = jnp.broadcast_to(m_new, m_ref.shape)

    needs_mask = (kj + 1) * tk > qi * tq
    pl.when(needs_mask)(lambda: step(True))
    pl.when(jnp.logical_not(needs_mask))(lambda: step(False))

    @pl.when(kj == last_kj)
    def _():
        o_ref[...] = (acc_ref[...] / l_ref[:, 0:1]).astype(o_ref.dtype)


def fox_attention(q, kv, c, c_rows, n_heads, tq=512, tk=512):
    S = q.shape[0]
    hd = FOX_HEAD
    tq, tk = min(tq, S), min(tk, S)
    assert tq % tk == 0
    pairs = [(qi, kj) for qi in range(S // tq) for kj in range((qi + 1) * (tq // tk))]
    qt = jnp.asarray([p[0] for p in pairs], jnp.int32)
    kt = jnp.asarray([p[1] for p in pairs], jnp.int32)
    grid_spec = pltpu.PrefetchScalarGridSpec(
        num_scalar_prefetch=2,
        grid=(n_heads, len(pairs)),
        in_specs=[
            pl.BlockSpec((tq, hd), lambda h, t, qt, kt: (qt[t], h)),
            pl.BlockSpec((tk, hd), lambda h, t, qt, kt: (kt[t], h)),
            pl.BlockSpec((tk, hd), lambda h, t, qt, kt: (kt[t], n_heads + h)),
            pl.BlockSpec((tq, V7X_LANES), lambda h, t, qt, kt: (qt[t], 0)),
            pl.BlockSpec((None, 1, tk), lambda h, t, qt, kt: (h, 0, kt[t])),
        ],
        out_specs=pl.BlockSpec((tq, hd), lambda h, t, qt, kt: (qt[t], h)),
        scratch_shapes=[pltpu.VMEM((tq, V7X_LANES), F32),
                        pltpu.VMEM((tq, V7X_LANES), F32),
                        pltpu.VMEM((tq, hd), F32),
                        pltpu.VMEM((tq, V7X_LANES), F32)],
    )
    return pl.pallas_call(
        functools.partial(_fox_kernel, tq=tq, tk=tk),
        grid_spec=grid_spec,
        out_shape=jax.ShapeDtypeStruct((S, n_heads * hd), BF16),
        compiler_params=_cparams("parallel", "arbitrary"),
        name="fox_attention",
    )(qt, kt, q, kv, kv, c, c_rows)


def _pad_cols(w, n):
    return jnp.pad(w, ((0, 0), (0, n - w.shape[1])))


def _pad_rows(w, n):
    return jnp.pad(w, ((0, n - w.shape[0]), (0, 0)))


def kernel(x, mix_norm, ffn_norm, final_norm, rwkv_x_mix, rwkv_w_rkv, rwkv_w0, rwkv_w1, rwkv_w2, rwkv_a0, rwkv_a1, rwkv_a2, rwkv_v0, rwkv_v1, rwkv_v2, rwkv_g1, rwkv_g2, rwkv_k_k, rwkv_k_a, rwkv_r_k, rwkv_lnx_w, rwkv_lnx_b, rwkv_w_o, kv_norm, w_kvf, b_f, fox_w_q, fox_w_o, mlp_w_up, mlp_w_down):
    B, S, D = x.shape
    assert B == 1
    n_a = rwkv_x_mix.shape[0]
    n_b = fox_w_q.shape[0]
    n_fox_heads = D // FOX_HEAD
    lora_pad = V7X_LANES
    xs = x.reshape(S, D)
    bf = lambda w: w.astype(BF16)

    v_first = None
    for i in range(n_a):
        mixed = rwkv_mix(xs, mix_norm[i:i + 1], rwkv_x_mix[i])
        rkv = matmul(mixed, bf(rwkv_w_rkv[i]), name="rkv_proj")
        hw = matmul(mixed, bf(_pad_cols(rwkv_w1[i], lora_pad)), x_idx=3, act=jnp.tanh,
                    out_dtype=BF16, name="lora_w")
        ha = matmul(mixed, bf(_pad_cols(rwkv_a1[i], lora_pad)), x_idx=4, out_dtype=BF16,
                    name="lora_a")
        hg = matmul(mixed, bf(rwkv_g1[i]), x_idx=5, act=_sigmoid, out_dtype=BF16, name="lora_g")
        w2 = bf(_pad_rows(rwkv_w2[i], lora_pad))
        a2 = bf(_pad_rows(rwkv_a2[i], lora_pad))
        g2 = bf(rwkv_g2[i])
        zeros = jnp.zeros((D,), F32)
        if i == 0:
            vecs = jnp.stack([rwkv_w0[i], rwkv_a0[i], zeros, rwkv_k_k[i], rwkv_k_a[i],
                              zeros, zeros, zeros])
            lw, kmod, a, b, g = rwkv_pre(rkv, hw, ha, None, hg, w2, a2, None, g2, vecs, None)
            v_first = rkv
            rec_v, rec_v_idx = rkv, 2
        else:
            vecs = jnp.stack([rwkv_w0[i], rwkv_a0[i], rwkv_v0[i - 1], rwkv_k_k[i], rwkv_k_a[i],
                              zeros, zeros, zeros])
            hv = matmul(mixed, bf(_pad_cols(rwkv_v1[i - 1], lora_pad)), x_idx=2, out_dtype=BF16,
                        name="lora_v")
            v2 = bf(_pad_rows(rwkv_v2[i - 1], lora_pad))
            lw, kmod, a, b, g, v_new = rwkv_pre(rkv, hw, ha, hv, hg, w2, a2, v2, g2, vecs,
                                                v_first)
            rec_v, rec_v_idx = v_new, None
        y = rwkv_recurrence(rkv, 0, lw, kmod, rec_v, rec_v_idx, a, b, g,
                            rwkv_lnx_w[i:i + 1], rwkv_lnx_b[i:i + 1],
                            rwkv_r_k[i].reshape(1, D))
        xs = matmul(y, bf(rwkv_w_o[i]), res=xs, name="rwkv_out")
        xs = mlp_block(xs, ffn_norm[i:i + 1], bf(mlp_w_up[i]), bf(mlp_w_down[i]))

    kv = c = c_rows = None
    for j in range(n_b):
        layer = n_a + j
        if j == 0:
            normed = rms_norm_multi(xs, jnp.stack([mix_norm[layer], kv_norm]), BF16)
            kv = matmul(normed, bf(w_kvf[:, :2 * D]), x_idx=1, out_dtype=BF16, name="kv_proj")
            b_pad = jnp.pad(b_f, (0, V7X_LANES - n_fox_heads)).reshape(1, V7X_LANES)
            log_f = matmul(normed, bf(_pad_cols(w_kvf[:, 2 * D:], V7X_LANES)), x_idx=1,
                           bias=b_pad, act=_log_sigmoid, name="forget_proj")
            c = cumsum_rows(log_f)
            c_rows = c[:, :n_fox_heads].T.reshape(n_fox_heads, 1, S)
        else:
            normed = rms_norm_multi(xs, mix_norm[layer:layer + 1], BF16)
        q = matmul(normed, bf(fox_w_q[j]), x_idx=0, scale=FOX_HEAD ** -0.5, out_dtype=BF16,
                   name="q_proj")
        o = fox_attention(q, kv, c, c_rows, n_fox_heads)
        xs = matmul(o, bf(fox_w_o[j]), res=xs, name="fox_out")
        xs = mlp_block(xs, ffn_norm[layer:layer + 1], bf(mlp_w_up[layer]), bf(mlp_w_down[layer]))

    out = rms_norm_multi(xs, final_norm.reshape(1, D), F32)
    return out.reshape(B, S, D)
```

```python
import functools
import math

import jax
import jax.numpy as jnp
from jax import lax
from jax.experimental import pallas as pl
from jax.experimental.pallas import tpu as pltpu

F32 = jnp.float32
BF16 = jnp.bfloat16

RWKV_HEAD = 64
FOX_HEAD = 128
RMS_EPS = 1e-6
LNX_EPS = 1e-5 * RWKV_HEAD
V7X_LANES = 128
REC_CHUNK = 64
V7X_VMEM_LIMIT = 56 * 1024 * 1024
MASK_VALUE = -1e30
LOG2_E = math.log2(math.e)


def _cparams(*sem):
    return pltpu.CompilerParams(dimension_semantics=sem, vmem_limit_bytes=V7X_VMEM_LIMIT)


def _sigmoid(z):
    return 1.0 / (1.0 + jnp.exp(-z))


def _log_sigmoid(z):
    return jnp.minimum(z, 0.0) - jnp.log1p(jnp.exp(-jnp.abs(z)))


def _rms(x, gain):
    return x * lax.rsqrt(jnp.mean(x * x, axis=-1, keepdims=True) + RMS_EPS) * gain


def _head_sum_matrix(n, head):
    r = lax.broadcasted_iota(jnp.int32, (n, n), 0) // head
    c = lax.broadcasted_iota(jnp.int32, (n, n), 1) // head
    return (r == c).astype(BF16)


def _group_sum(x, ones_bd):
    hi = x.astype(BF16)
    lo = (x - hi.astype(F32)).astype(BF16)
    return (jnp.dot(hi, ones_bd, preferred_element_type=F32)
            + jnp.dot(lo, ones_bd, preferred_element_type=F32))


def _rms_kernel(x_ref, g_ref, o_ref):
    x = x_ref[...]
    y = x * lax.rsqrt(jnp.mean(x * x, axis=-1, keepdims=True) + RMS_EPS)
    for n in range(o_ref.shape[0]):
        o_ref[n] = (y * g_ref[n:n + 1, :]).astype(o_ref.dtype)


def rms_norm_multi(x, gains, out_dtype, tm=512):
    S, D = x.shape
    G = gains.shape[0]
    tm = min(tm, S)
    return pl.pallas_call(
        _rms_kernel,
        grid=(S // tm,),
        in_specs=[pl.BlockSpec((tm, D), lambda i: (i, 0)),
                  pl.BlockSpec((G, D), lambda i: (0, 0))],
        out_specs=pl.BlockSpec((G, tm, D), lambda i: (0, i, 0)),
        out_shape=jax.ShapeDtypeStruct((G, S, D), out_dtype),
        compiler_params=_cparams("parallel"),
        name="rms_norm",
    )(x, gains)


def _mix_kernel(x_ref, halo_ref, g_ref, mix_ref, o_ref):
    i = pl.program_id(0)
    gain = g_ref[...]
    h = _rms(x_ref[...], gain)
    hh = _rms(halo_ref[...], gain)
    last = jnp.where(i == 0, 0.0, hh[7:8, :])
    row = lax.broadcasted_iota(jnp.int32, h.shape, 0)
    prev = jnp.where(row == 0, last, pltpu.roll(h, 1, 0))
    d = prev - h
    for n in range(o_ref.shape[0]):
        o_ref[n] = (h + d * mix_ref[n:n + 1, :]).astype(o_ref.dtype)


def rwkv_mix(x, gain, x_mix, tm=256):
    S, D = x.shape
    n_mix = x_mix.shape[0]
    tm = min(tm, S)
    hb = tm // 8
    return pl.pallas_call(
        _mix_kernel,
        grid=(S // tm,),
        in_specs=[pl.BlockSpec((tm, D), lambda i: (i, 0)),
                  pl.BlockSpec((8, D), lambda i: (jnp.maximum(i * hb - 1, 0), 0)),
                  pl.BlockSpec((1, D), lambda i: (0, 0)),
                  pl.BlockSpec((n_mix, D), lambda i: (0, 0))],
        out_specs=pl.BlockSpec((n_mix, tm, D), lambda i: (0, i, 0)),
        out_shape=jax.ShapeDtypeStruct((n_mix, S, D), BF16),
        compiler_params=_cparams("parallel"),
        name="rwkv_mix",
    )(x, x, gain, x_mix)


def _mm_kernel(x_ref, w_ref, *rest, act, scale, has_bias, has_res):
    o_ref = rest[-1]
    acc = jnp.dot(x_ref[...], w_ref[...], preferred_element_type=F32)
    if scale != 1.0:
        acc = acc * scale
    if has_bias:
        acc = acc + rest[0][...]
    if act is not None:
        acc = act(acc)
    if has_res:
        acc = acc + rest[-2][...]
    o_ref[...] = acc.astype(o_ref.dtype)


def matmul(x, w, *, x_idx=None, act=None, scale=1.0, bias=None, res=None, out_dtype=F32,
           tm=1024, tn=1024, name="matmul"):
    batched = w.ndim == 3
    M, K = x.shape[-2:]
    N = w.shape[-1]
    tm, tn = min(tm, M), min(tn, N)
    kern = functools.partial(_mm_kernel, act=act, scale=scale, has_bias=bias is not None,
                             has_res=res is not None)
    if batched:
        nb = w.shape[0]
        grid = (nb, M // tm, N // tn)
        x_spec = pl.BlockSpec((None, tm, K), lambda b, i, j: (b, i, 0))
        w_spec = pl.BlockSpec((None, K, tn), lambda b, i, j: (b, 0, j))
        o_spec = pl.BlockSpec((None, tm, tn), lambda b, i, j: (b, i, j))
        out_shape = jax.ShapeDtypeStruct((nb, M, N), out_dtype)
        sem = ("parallel", "parallel", "parallel")
    else:
        grid = (M // tm, N // tn)
        if x_idx is None:
            x_spec = pl.BlockSpec((tm, K), lambda i, j: (i, 0))
        else:
            x_spec = pl.BlockSpec((None, tm, K), lambda i, j: (x_idx, i, 0))
        w_spec = pl.BlockSpec((K, tn), lambda i, j: (0, j))
        o_spec = pl.BlockSpec((tm, tn), lambda i, j: (i, j))
        out_shape = jax.ShapeDtypeStruct((M, N), out_dtype)
        sem = ("parallel", "parallel")
    in_specs = [x_spec, w_spec]
    args = [x, w]
    if bias is not None:
        assert not batched
        in_specs.append(pl.BlockSpec((1, tn), lambda i, j: (0, j)))
        args.append(bias)
    if res is not None:
        assert not batched
        in_specs.append(o_spec)
        args.append(res)
    return pl.pallas_call(
        kern, grid=grid, in_specs=in_specs, out_specs=o_spec, out_shape=out_shape,
        compiler_params=_cparams(*sem), name=name,
    )(*args)


def _pre_kernel(*refs, has_vres):
    if has_vres:
        (r_ref, k_ref, v_ref, hw_ref, ha_ref, hv_ref, hg_ref, w2_ref, a2_ref, v2_ref, g2_ref,
         vec_ref, vf_ref, *outs) = refs
    else:
        (r_ref, k_ref, v_ref, hw_ref, ha_ref, hg_ref, w2_ref, a2_ref, g2_ref,
         vec_ref, *outs) = refs
    ao_ref, ro_ref, bo_ref, ko_ref, bdo_ref, kdo_ref, vo_ref, pend_ref, bonus_ref, g_ref = outs
    tm, D = k_ref.shape
    T, W = REC_CHUNK, V7X_LANES
    ri = lax.broadcasted_iota(jnp.int32, (tm, tm), 0)
    ci = lax.broadcasted_iota(jnp.int32, (tm, tm), 1)
    tri = jnp.logical_and(ci <= ri, ci // T == ri // T).astype(F32)
    row = lax.broadcasted_iota(jnp.int32, (tm, W), 0)
    ones_bd = _head_sum_matrix(W, RWKV_HEAD)
    hw, ha, hg = hw_ref[...], ha_ref[...], hg_ref[...]

    for s in range(D // W):
        sl = slice(s * W, (s + 1) * W)
        w0, a0, v0, k_k, k_a, r_k = (vec_ref[n:n + 1, sl] for n in range(6))
        r, k, v = r_ref[:, sl], k_ref[:, sl], v_ref[:, sl]
        z_w = w0 + jnp.dot(hw, w2_ref[:, sl], preferred_element_type=F32)
        lw = -jnp.exp(_log_sigmoid(z_w) - 0.5)
        a_sig = _sigmoid(a0 + jnp.dot(ha, a2_ref[:, sl], preferred_element_type=F32))
        g_ref[:, sl] = jnp.dot(hg, g2_ref[:, sl], preferred_element_type=F32)
        if has_vres:
            mix = _sigmoid(v0 + jnp.dot(hv_ref[...], v2_ref[:, sl], preferred_element_type=F32))
            v = v + (vf_ref[:, sl] - v) * mix
        kmod = k * (1.0 + (a_sig - 1.0) * k_a)
        kk = k * k_k
        kk = kk / jnp.maximum(jnp.sqrt(_group_sum(kk * kk, ones_bd)), 1e-12)
        b = kk * a_sig

        L = jnp.dot(tri, lw, precision=lax.Precision.HIGHEST, preferred_element_type=F32)
        l_end = L[T - 1:T, :]
        for c in range(1, tm // T):
            l_end = jnp.where(row < c * T, l_end, L[(c + 1) * T - 1:(c + 1) * T, :])
        for c in range(tm // T):
            pend_ref[c, :, sl] = jnp.broadcast_to(jnp.exp(L[(c + 1) * T - 1:(c + 1) * T, :]), (8, W))
        e_neg = jnp.exp(-L)
        to_end = jnp.exp(l_end - L)
        ao_ref[:, sl] = (-kk * jnp.exp(L - lw)).astype(BF16)
        ro_ref[:, sl] = (r * jnp.exp(L)).astype(BF16)
        bo_ref[:, sl] = (b * e_neg).astype(BF16)
        ko_ref[:, sl] = (kmod * e_neg).astype(BF16)
        bdo_ref[:, sl] = (b * to_end).astype(BF16)
        kdo_ref[:, sl] = (kmod * to_end).astype(BF16)
        vo_ref[:, sl] = v.astype(BF16)
        bonus_ref[:, sl] = _group_sum(r * kmod * r_k, ones_bd) * v


def rwkv_pre(rkv, hw, ha, hv, hg, w2, a2, v2, g2, vecs, v_first, tm=128):
    _, S, D = rkv.shape
    tm = min(tm, S)
    assert tm % REC_CHUNK == 0
    has_vres = v_first is not None
    row = lambda i: (i, 0)
    full = lambda i: (0, 0)
    act = pl.BlockSpec((tm, D), row)
    rkv_spec = lambda n: pl.BlockSpec((None, tm, D), lambda i: (n, i, 0))
    lora = lambda a: pl.BlockSpec((tm, a.shape[1]), row)
    wspec = lambda a: pl.BlockSpec(a.shape, full)
    if has_vres:
        in_specs = [rkv_spec(0), rkv_spec(1), rkv_spec(2), lora(hw), lora(ha), lora(hv), lora(hg),
                    wspec(w2), wspec(a2), wspec(v2), wspec(g2), wspec(vecs), rkv_spec(2)]
        args = [rkv, rkv, rkv, hw, ha, hv, hg, w2, a2, v2, g2, vecs, v_first]
    else:
        in_specs = [rkv_spec(0), rkv_spec(1), rkv_spec(2), lora(hw), lora(ha), lora(hg),
                    wspec(w2), wspec(a2), wspec(g2), wspec(vecs)]
        args = [rkv, rkv, rkv, hw, ha, hg, w2, a2, g2, vecs]
    n_chunks = S // REC_CHUNK
    pend_spec = pl.BlockSpec((tm // REC_CHUNK, 8, D), lambda i: (i, 0, 0))
    bf_out = jax.ShapeDtypeStruct((S, D), BF16)
    f32_out = jax.ShapeDtypeStruct((S, D), F32)
    return pl.pallas_call(
        functools.partial(_pre_kernel, has_vres=has_vres),
        grid=(S // tm,),
        in_specs=in_specs,
        out_specs=[act] * 7 + [pend_spec, act, act],
        out_shape=[bf_out] * 7 + [jax.ShapeDtypeStruct((n_chunks, 8, D), F32), f32_out, f32_out],
        compiler_params=_cparams("parallel"),
        name="rwkv_pre",
    )(*args)


def _rec_kernel(a_ref, r_ref, b_ref, k_ref, bd_ref, kd_ref, v_ref, pend_ref, y_ref, st_ref):
    T, W = REC_CHUNK, V7X_LANES
    D = a_ref.shape[-1]
    slabs = range(D // W)

    @pl.when(pl.program_id(0) == 0)
    def _():
        st_ref[...] = jnp.zeros_like(st_ref)

    head0 = lax.broadcasted_iota(jnp.int32, (T, W), 1) < RWKV_HEAD
    rr = lax.broadcasted_iota(jnp.int32, (2 * T, 2 * T), 0) & (T - 1)
    cc = lax.broadcasted_iota(jnp.int32, (2 * T, 2 * T), 1) & (T - 1)
    strict = cc < rr
    incl = cc <= rr

    def stack(ref, p):
        x = ref[:, p * W:(p + 1) * W].astype(F32)
        return jnp.concatenate([jnp.where(head0, x, 0.0), jnp.where(head0, 0.0, x)],
                               axis=0).astype(BF16)

    def mm(x, y):
        return jnp.dot(x.astype(BF16), y.astype(BF16), preferred_element_type=F32)

    def mm_nt(x, y):
        return lax.dot_general(x.astype(BF16), y.astype(BF16), (((1,), (1,)), ((), ())),
                               preferred_element_type=F32)

    def mm_tn(x, y):
        return lax.dot_general(x.astype(BF16), y.astype(BF16), (((0,), (0,)), ((), ())),
                               preferred_element_type=F32)

    a_s = [stack(a_ref, p) for p in slabs]
    r_s = [stack(r_ref, p) for p in slabs]
    v_s = [stack(v_ref, p) for p in slabs]
    gram = [mm_nt(jnp.concatenate([a_s[p], r_s[p]], axis=0),
                  jnp.concatenate([stack(b_ref, p), stack(k_ref, p)], axis=0)) for p in slabs]
    n_pow = [jnp.where(strict, g[:2 * T, :2 * T], 0.0).astype(BF16) for g in gram]
    a_rb = [jnp.where(incl, g[2 * T:, :2 * T], 0.0).astype(BF16) for g in gram]
    av = [mm(jnp.concatenate([jnp.where(strict, g[:2 * T, 2 * T:], 0.0),
                              jnp.where(incl, g[2 * T:, 2 * T:], 0.0)], axis=0), v_s[p])
          for p, g in zip(slabs, gram)]

    x = [jnp.concatenate([a_s[p].astype(F32), av[p][:2 * T]], axis=1) for p in slabs]
    x = [x[p] + mm(n_pow[p], x[p]) for p in slabs]
    for _ in range(5):
        n_pow = [mm(n, n).astype(BF16) for n in n_pow]
        x = [x[p] + mm(n_pow[p], x[p]) for p in slabs]

    s0 = [st_ref[p] for p in slabs]
    wr = [mm_nt(jnp.concatenate([x[p][:, :W].astype(BF16), r_s[p]], axis=0), s0[p])
          for p in slabs]
    u_s = [wr[p][:2 * T] + x[p][:, W:] for p in slabs]
    for p in slabs:
        st_ref[p] = (s0[p] * pend_ref[0:1, p * W:(p + 1) * W]
                     + mm_tn(jnp.concatenate([u_s[p].astype(BF16), v_s[p]], axis=0),
                             jnp.concatenate([stack(bd_ref, p), stack(kd_ref, p)], axis=0)))
    for p in slabs:
        y_s = wr[p][2 * T:] + mm(a_rb[p], u_s[p]) + av[p][2 * T:]
        y_ref[:, p * W:(p + 1) * W] = y_s[:T] + y_s[T:]


def rwkv_recurrence(a, r, b, k, bd, kd, v, pend):
    S, D = a.shape
    T = REC_CHUNK
    act = pl.BlockSpec((T, D), lambda c: (c, 0))
    return pl.pallas_call(
        _rec_kernel,
        grid=(S // T,),
        in_specs=[act] * 7 + [pl.BlockSpec((None, 8, D), lambda c: (c, 0, 0))],
        out_specs=act,
        out_shape=jax.ShapeDtypeStruct((S, D), F32),
        scratch_shapes=[pltpu.VMEM((D // V7X_LANES, V7X_LANES, V7X_LANES), F32)],
        compiler_params=_cparams("arbitrary"),
        name="rwkv_recurrence",
    )(a, r, b, k, bd, kd, v, pend)


def _post_kernel(y_ref, bonus_ref, g_ref, lnw_ref, lnb_ref, o_ref):
    W = V7X_LANES
    ones_bd = _head_sum_matrix(W, RWKV_HEAD)
    inv_n = 1.0 / RWKV_HEAD
    for s in range(y_ref.shape[-1] // W):
        sl = slice(s * W, (s + 1) * W)
        y = y_ref[:, sl]
        d = y - _group_sum(y, ones_bd) * inv_n
        var = _group_sum(d * d, ones_bd) * inv_n
        yn = d * lax.rsqrt(var + LNX_EPS) * lnw_ref[:, sl] + lnb_ref[:, sl]
        o_ref[:, sl] = ((yn + bonus_ref[:, sl]) * g_ref[:, sl]).astype(o_ref.dtype)


def rwkv_post(y, bonus, g, lnw, lnb, tm=256):
    S, D = y.shape
    tm = min(tm, S)
    act = pl.BlockSpec((tm, D), lambda i: (i, 0))
    vec = pl.BlockSpec((1, D), lambda i: (0, 0))
    return pl.pallas_call(
        _post_kernel,
        grid=(S // tm,),
        in_specs=[act, act, act, vec, vec],
        out_specs=act,
        out_shape=jax.ShapeDtypeStruct((S, D), BF16),
        compiler_params=_cparams("parallel"),
        name="rwkv_post",
    )(y, bonus, g, lnw, lnb)


def _mlp_kernel(x_ref, g_ref, wu_ref, wd_ref, o_ref, xn_ref):
    @pl.when(pl.program_id(1) == 0)
    def _():
        x = x_ref[...]
        xn_ref[...] = _rms(x, g_ref[...]).astype(BF16)
        o_ref[...] = x

    h = jnp.dot(xn_ref[...], wu_ref[...], preferred_element_type=F32)
    h = jnp.square(jnp.maximum(h, 0.0)).astype(BF16)
    o_ref[...] += jnp.dot(h, wd_ref[...], preferred_element_type=F32)


def mlp_block(x, gain, w_up, w_down, tm=512, tf=1024):
    S, D = x.shape
    F = w_up.shape[1]
    tm, tf = min(tm, S), min(tf, F)
    return pl.pallas_call(
        _mlp_kernel,
        grid=(S // tm, F // tf),
        in_specs=[pl.BlockSpec((tm, D), lambda i, f: (i, 0)),
                  pl.BlockSpec((1, D), lambda i, f: (0, 0)),
                  pl.BlockSpec((D, tf), lambda i, f: (0, f)),
                  pl.BlockSpec((tf, D), lambda i, f: (f, 0))],
        out_specs=pl.BlockSpec((tm, D), lambda i, f: (i, 0)),
        out_shape=jax.ShapeDtypeStruct((S, D), F32),
        scratch_shapes=[pltpu.VMEM((tm, D), BF16)],
        compiler_params=_cparams("parallel", "arbitrary"),
        name="mlp",
    )(x, gain, w_up, w_down)


def _cumsum_kernel(x_ref, o_ref, carry_ref):
    @pl.when(pl.program_id(0) == 0)
    def _():
        carry_ref[...] = jnp.zeros_like(carry_ref)

    n = x_ref.shape[0]
    tri = (lax.broadcasted_iota(jnp.int32, (n, n), 1)
           <= lax.broadcasted_iota(jnp.int32, (n, n), 0)).astype(F32)
    c = jnp.dot(tri, x_ref[...], precision=lax.Precision.HIGHEST,
                preferred_element_type=F32) + carry_ref[0:1, :]
    o_ref[...] = c
    carry_ref[...] = jnp.broadcast_to(c[n - 1:n, :], carry_ref.shape)


def cumsum_rows(x, tc=256):
    S, W = x.shape
    tc = min(tc, S)
    return pl.pallas_call(
        _cumsum_kernel,
        grid=(S // tc,),
        in_specs=[pl.BlockSpec((tc, W), lambda i: (i, 0))],
        out_specs=pl.BlockSpec((tc, W), lambda i: (i, 0)),
        out_shape=jax.ShapeDtypeStruct((S, W), F32),
        scratch_shapes=[pltpu.VMEM((8, W), F32)],
        compiler_params=_cparams("arbitrary"),
        name="forget_cumsum",
    )(x)


FOX_ONES_ROWS = 16


def _fox_kernel(q_ref, k_ref, vt_ref, c_ref, o_ref, ck_ref, m_ref, acc_ref, *, tq, tk):
    h = pl.program_id(0)
    qi = pl.program_id(1)
    nsub = tq // tk
    hd = k_ref.shape[1]
    S = k_ref.shape[0]

    @pl.when(qi == 0)
    def _():
        lane = lax.broadcasted_iota(jnp.int32, (tk, V7X_LANES), 1)
        for j in range(S // tk):
            col = jnp.sum(jnp.where(lane == h, c_ref[j * tk:(j + 1) * tk, :], 0.0),
                          axis=1, keepdims=True)
            ck_ref[j * tk:(j + 1) * tk, :] = jnp.broadcast_to(col * LOG2_E, (tk, V7X_LANES))

    m_ref[...] = jnp.full_like(m_ref, MASK_VALUE)
    acc_ref[...] = jnp.zeros_like(acc_ref)
    q = q_ref[...]
    ones_rows = jnp.ones((FOX_ONES_ROWS, tk), BF16)

    def block(kj, masked):
        start = pl.multiple_of(kj * tk, tk)
        s = lax.dot_general(k_ref[pl.ds(start, tk), :], q, (((1,), (1,)), ((), ())),
                            preferred_element_type=F32)
        ck = ck_ref[pl.ds(start, tk), :]
        s = s - jnp.concatenate([ck] * (tq // V7X_LANES), axis=1)
        if masked:
            key = kj * tk + lax.broadcasted_iota(jnp.int32, (tk, tq), 0)
            qry = qi * tq + lax.broadcasted_iota(jnp.int32, (tk, tq), 1)
            s = jnp.where(qry >= key, s, MASK_VALUE)
        m_prev = m_ref[0:1, :]
        m_new = jnp.maximum(m_prev, jnp.max(s, axis=0, keepdims=True))
        alpha = jnp.exp2(m_prev - m_new)
        p = jnp.exp2(s - m_new).astype(BF16)
        vt = jnp.concatenate([vt_ref[kj], ones_rows], axis=0)
        acc_ref[...] = alpha * acc_ref[...] + jnp.dot(vt, p, preferred_element_type=F32)
        m_ref[0:1, :] = m_new

    def full_blocks(i, carry):
        for u in range(nsub):
            block(i * nsub + u, False)
        return carry

    lax.fori_loop(0, qi, full_blocks, 0)
    for u in range(nsub):
        block(qi * nsub + u, True)
    out_t = acc_ref[0:hd, :] / acc_ref[hd:hd + 1, :]
    o_ref[...] = out_t.T.astype(o_ref.dtype)


def fox_attention(q, k, vt, c, n_heads, tq=1024, tk=512):
    S = q.shape[0]
    hd = FOX_HEAD
    tq = min(tq, S)
    assert tq % tk == 0 and vt.shape[-1] == tk
    return pl.pallas_call(
        functools.partial(_fox_kernel, tq=tq, tk=tk),
        grid=(n_heads, S // tq),
        in_specs=[pl.BlockSpec((tq, hd), lambda h, i: (i, h)),
                  pl.BlockSpec((S, hd), lambda h, i: (0, h)),
                  pl.BlockSpec((None, S // tk, hd, tk), lambda h, i: (h, 0, 0, 0)),
                  pl.BlockSpec((S, V7X_LANES), lambda h, i: (0, 0))],
        out_specs=pl.BlockSpec((tq, hd), lambda h, i: (i, h)),
        out_shape=jax.ShapeDtypeStruct((S, n_heads * hd), BF16),
        scratch_shapes=[pltpu.VMEM((S, V7X_LANES), F32),
                        pltpu.VMEM((8, tq), F32),
                        pltpu.VMEM((hd + FOX_ONES_ROWS, tq), F32)],
        compiler_params=_cparams("parallel", "arbitrary"),
        name="fox_attention",
    )(q, k, vt, c)


def _pad_cols(w, n):
    return jnp.pad(w, ((0, 0), (0, n - w.shape[1])))


def _pad_rows(w, n):
    return jnp.pad(w, ((0, n - w.shape[0]), (0, 0)))


def kernel(x, mix_norm, ffn_norm, final_norm, rwkv_x_mix, rwkv_w_rkv, rwkv_w0, rwkv_w1, rwkv_w2, rwkv_a0, rwkv_a1, rwkv_a2, rwkv_v0, rwkv_v1, rwkv_v2, rwkv_g1, rwkv_g2, rwkv_k_k, rwkv_k_a, rwkv_r_k, rwkv_lnx_w, rwkv_lnx_b, rwkv_w_o, kv_norm, w_kvf, b_f, fox_w_q, fox_w_o, mlp_w_up, mlp_w_down):
    B, S, D = x.shape
    assert B == 1
    n_a = rwkv_x_mix.shape[0]
    n_b = fox_w_q.shape[0]
    n_fox_heads = D // FOX_HEAD
    lora_pad = V7X_LANES
    fox_tk = min(512, S)
    xs = x.reshape(S, D)
    bf = lambda w: w.astype(BF16)

    v_first = None
    for i in range(n_a):
        mixed = rwkv_mix(xs, mix_norm[i:i + 1], rwkv_x_mix[i])
        rkv = matmul(mixed, bf(rwkv_w_rkv[i]), name="rkv_proj")
        hw = matmul(mixed, bf(_pad_cols(rwkv_w1[i], lora_pad)), x_idx=3, act=jnp.tanh,
                    out_dtype=BF16, name="lora_w")
        ha = matmul(mixed, bf(_pad_cols(rwkv_a1[i], lora_pad)), x_idx=4, out_dtype=BF16,
                    name="lora_a")
        hg = matmul(mixed, bf(rwkv_g1[i]), x_idx=5, act=_sigmoid, out_dtype=BF16, name="lora_g")
        w2 = bf(_pad_rows(rwkv_w2[i], lora_pad))
        a2 = bf(_pad_rows(rwkv_a2[i], lora_pad))
        g2 = bf(rwkv_g2[i])
        zeros = jnp.zeros((D,), F32)
        r_k = rwkv_r_k[i].reshape(D)
        if i == 0:
            vecs = jnp.stack([rwkv_w0[i], rwkv_a0[i], zeros, rwkv_k_k[i], rwkv_k_a[i], r_k,
                              zeros, zeros])
            pre = rwkv_pre(rkv, hw, ha, None, hg, w2, a2, None, g2, vecs, None)
            v_first = rkv
        else:
            vecs = jnp.stack([rwkv_w0[i], rwkv_a0[i], rwkv_v0[i - 1], rwkv_k_k[i], rwkv_k_a[i],
                              r_k, zeros, zeros])
            hv = matmul(mixed, bf(_pad_cols(rwkv_v1[i - 1], lora_pad)), x_idx=2, out_dtype=BF16,
                        name="lora_v")
            v2 = bf(_pad_rows(rwkv_v2[i - 1], lora_pad))
            pre = rwkv_pre(rkv, hw, ha, hv, hg, w2, a2, v2, g2, vecs, v_first)
        *rec_in, bonus, g = pre
        y = rwkv_recurrence(*rec_in)
        y = rwkv_post(y, bonus, g, rwkv_lnx_w[i:i + 1], rwkv_lnx_b[i:i + 1])
        xs = matmul(y, bf(rwkv_w_o[i]), res=xs, name="rwkv_out")
        xs = mlp_block(xs, ffn_norm[i:i + 1], bf(mlp_w_up[i]), bf(mlp_w_down[i]))

    kv = vt = c = None
    for j in range(n_b):
        layer = n_a + j
        if j == 0:
            normed = rms_norm_multi(xs, jnp.stack([mix_norm[layer], kv_norm]), BF16)
            kv = matmul(normed, bf(w_kvf[:, :2 * D]), x_idx=1, out_dtype=BF16, name="kv_proj")
            b_pad = jnp.pad(b_f, (0, V7X_LANES - n_fox_heads)).reshape(1, V7X_LANES)
            log_f = matmul(normed, bf(_pad_cols(w_kvf[:, 2 * D:], V7X_LANES)), x_idx=1,
                           bias=b_pad, act=_log_sigmoid, name="forget_proj")
            c = cumsum_rows(log_f)
            vt = kv[:, D:].reshape(S // fox_tk, fox_tk, n_fox_heads, FOX_HEAD).transpose(2, 0, 3, 1)
        else:
            normed = rms_norm_multi(xs, mix_norm[layer:layer + 1], BF16)
        q = matmul(normed, bf(fox_w_q[j]), x_idx=0, scale=LOG2_E * FOX_HEAD ** -0.5,
                   out_dtype=BF16, name="q_proj")
        o = fox_attention(q, kv, vt, c, n_fox_heads, tk=fox_tk)
        xs = matmul(o, bf(fox_w_o[j]), res=xs, name="fox_out")
        xs = mlp_block(xs, ffn_norm[layer:layer + 1], bf(mlp_w_up[layer]), bf(mlp_w_down[layer]))

    out = rms_norm_multi(xs, final_norm.reshape(1, D), F32)
    return out.reshape(B, S, D)
```

```python
import functools
import math

import jax
import jax.numpy as jnp
from jax import lax
from jax.experimental import pallas as pl
from jax.experimental.pallas import tpu as pltpu

F32 = jnp.float32
BF16 = jnp.bfloat16

RWKV_HEAD = 64
FOX_HEAD = 128
RMS_EPS = 1e-6
LNX_EPS = 1e-5 * RWKV_HEAD
V7X_LANES = 128
REC_CHUNK = 64
V7X_VMEM_LIMIT = 56 * 1024 * 1024
MASK_VALUE = -1e30
LOG2_E = math.log2(math.e)
DECAY_SCALE = math.exp(-0.5)


def _cparams(*sem):
    return pltpu.CompilerParams(dimension_semantics=sem, vmem_limit_bytes=V7X_VMEM_LIMIT)


def _sigmoid(z):
    return 1.0 / (1.0 + jnp.exp(-z))


def _log_sigmoid(z):
    return jnp.minimum(z, 0.0) - jnp.log(1.0 + jnp.exp(-jnp.abs(z)))


def _rms(x, gain):
    return x * lax.rsqrt(jnp.mean(x * x, axis=-1, keepdims=True) + RMS_EPS) * gain


def _head_sum_matrix(n, head):
    r = lax.broadcasted_iota(jnp.int32, (n, n), 0) // head
    c = lax.broadcasted_iota(jnp.int32, (n, n), 1) // head
    return (r == c).astype(BF16)


def _group_sum(x, ones_bd):
    return jnp.dot(x.astype(BF16), ones_bd, preferred_element_type=F32)


def _tri_sum(tri, x):
    hi = x.astype(BF16)
    r1 = x - hi.astype(F32)
    mid = r1.astype(BF16)
    lo = (r1 - mid.astype(F32)).astype(BF16)
    return (jnp.dot(tri, hi, preferred_element_type=F32)
            + jnp.dot(tri, mid, preferred_element_type=F32)
            + jnp.dot(tri, lo, preferred_element_type=F32))


def _rms_kernel(x_ref, g_ref, o_ref):
    x = x_ref[...]
    y = x * lax.rsqrt(jnp.mean(x * x, axis=-1, keepdims=True) + RMS_EPS)
    for n in range(o_ref.shape[0]):
        o_ref[n] = (y * g_ref[n:n + 1, :]).astype(o_ref.dtype)


def rms_norm_multi(x, gains, out_dtype, tm=512):
    S, D = x.shape
    G = gains.shape[0]
    tm = min(tm, S)
    return pl.pallas_call(
        _rms_kernel,
        grid=(S // tm,),
        in_specs=[pl.BlockSpec((tm, D), lambda i: (i, 0)),
                  pl.BlockSpec((G, D), lambda i: (0, 0))],
        out_specs=pl.BlockSpec((G, tm, D), lambda i: (0, i, 0)),
        out_shape=jax.ShapeDtypeStruct((G, S, D), out_dtype),
        compiler_params=_cparams("parallel"),
        name="rms_norm",
    )(x, gains)


def _mix_kernel(x_ref, halo_ref, g_ref, mix_ref, o_ref):
    i = pl.program_id(0)
    gain = g_ref[...]
    h = _rms(x_ref[...], gain)
    hh = _rms(halo_ref[...], gain)
    last = jnp.where(i == 0, 0.0, hh[7:8, :])
    row = lax.broadcasted_iota(jnp.int32, h.shape, 0)
    prev = jnp.where(row == 0, last, pltpu.roll(h, 1, 0))
    d = prev - h
    for n in range(o_ref.shape[0]):
        o_ref[n] = (h + d * mix_ref[n:n + 1, :]).astype(o_ref.dtype)


def rwkv_mix(x, gain, x_mix, tm=256):
    S, D = x.shape
    n_mix = x_mix.shape[0]
    tm = min(tm, S)
    hb = tm // 8
    return pl.pallas_call(
        _mix_kernel,
        grid=(S // tm,),
        in_specs=[pl.BlockSpec((tm, D), lambda i: (i, 0)),
                  pl.BlockSpec((8, D), lambda i: (jnp.maximum(i * hb - 1, 0), 0)),
                  pl.BlockSpec((1, D), lambda i: (0, 0)),
                  pl.BlockSpec((n_mix, D), lambda i: (0, 0))],
        out_specs=pl.BlockSpec((n_mix, tm, D), lambda i: (0, i, 0)),
        out_shape=jax.ShapeDtypeStruct((n_mix, S, D), BF16),
        compiler_params=_cparams("parallel"),
        name="rwkv_mix",
    )(x, x, gain, x_mix)


def _mm_kernel(x_ref, w_ref, *rest, act, scale, has_bias, has_res, cast_w, row_axis):
    if cast_w:
        o_ref, wbf_ref = rest[-2], rest[-1]

        @pl.when(pl.program_id(row_axis) == 0)
        def _():
            wbf_ref[...] = w_ref[...].astype(BF16)
        w = wbf_ref[...]
    else:
        o_ref = rest[-1]
        w = w_ref[...]
    acc = jnp.dot(x_ref[...], w, preferred_element_type=F32)
    if scale != 1.0:
        acc = acc * scale
    if has_bias:
        acc = acc + rest[0][...]
    if act is not None:
        acc = act(acc)
    if has_res:
        acc = acc + rest[1 if has_bias else 0][...]
    o_ref[...] = acc.astype(o_ref.dtype)


def matmul(x, w, *, x_idx=None, w_idx=(), n_out=None, act=None, scale=1.0, bias=None, res=None,
           out_dtype=F32, tm=1024, tn=1024, name="matmul"):
    M, K = x.shape[-2:]
    N = w.shape[-1] if n_out is None else n_out
    batched = w.ndim - len(w_idx) == 3
    tm, tn = min(tm, M), min(tn, N)
    cast_w = w.dtype != BF16
    lead = (None,) * len(w_idx)
    if batched:
        nb = w.shape[len(w_idx)]
        grid = (nb, N // tn, M // tm)
        x_spec = pl.BlockSpec((None, tm, K), lambda b, j, i: (b, i, 0))
        w_spec = pl.BlockSpec(lead + (None, K, tn), lambda b, j, i: w_idx + (b, 0, j))
        o_spec = pl.BlockSpec((None, tm, tn), lambda b, j, i: (b, i, j))
        out_shape = jax.ShapeDtypeStruct((nb, M, N), out_dtype)
        sem = ("parallel", "parallel", "arbitrary")
    else:
        grid = (N // tn, M // tm)
        if x_idx is None:
            x_spec = pl.BlockSpec((tm, K), lambda j, i: (i, 0))
        else:
            x_spec = pl.BlockSpec((None, tm, K), lambda j, i: (x_idx, i, 0))
        w_spec = pl.BlockSpec(lead + (K, tn), lambda j, i: w_idx + (0, j))
        o_spec = pl.BlockSpec((tm, tn), lambda j, i: (i, j))
        out_shape = jax.ShapeDtypeStruct((M, N), out_dtype)
        sem = ("parallel", "arbitrary")
    kern = functools.partial(_mm_kernel, act=act, scale=scale, has_bias=bias is not None,
                             has_res=res is not None, cast_w=cast_w, row_axis=len(grid) - 1)
    in_specs = [x_spec, w_spec]
    args = [x, w]
    if bias is not None:
        assert not batched
        in_specs.append(pl.BlockSpec((1, tn), lambda j, i: (0, j)))
        args.append(bias)
    if res is not None:
        assert not batched
        in_specs.append(o_spec)
        args.append(res)
    return pl.pallas_call(
        kern, grid=grid, in_specs=in_specs, out_specs=o_spec, out_shape=out_shape,
        scratch_shapes=[pltpu.VMEM((K, tn), BF16)] if cast_w else [],
        compiler_params=_cparams(*sem), name=name,
    )(*args)


def _pre_kernel(*refs, has_vres):
    if has_vres:
        (r_ref, k_ref, v_ref, hw_ref, ha_ref, hv_ref, hg_ref, w2_ref, a2_ref, v2_ref, g2_ref,
         vec_ref, vf_ref, *outs) = refs
    else:
        (r_ref, k_ref, v_ref, hw_ref, ha_ref, hg_ref, w2_ref, a2_ref, g2_ref,
         vec_ref, *outs) = refs
    ao_ref, ro_ref, bo_ref, ko_ref, bdo_ref, kdo_ref, vo_ref, pend_ref, bonus_ref, g_ref = outs
    tm, D = k_ref.shape
    T, W = REC_CHUNK, V7X_LANES
    ri = lax.broadcasted_iota(jnp.int32, (tm, tm), 0)
    ci = lax.broadcasted_iota(jnp.int32, (tm, tm), 1)
    tri = jnp.logical_and(ci <= ri, ci // T == ri // T).astype(BF16)
    row = lax.broadcasted_iota(jnp.int32, (tm, D), 0)
    ones_bd = _head_sum_matrix(W, RWKV_HEAD)
    w0, a0, v0, k_k, k_a, r_k = (vec_ref[n:n + 1, :] for n in range(6))

    def head_sum(x):
        return jnp.concatenate([_group_sum(x[:, s * W:(s + 1) * W], ones_bd)
                                for s in range(D // W)], axis=1)

    r, k, v = r_ref[...], k_ref[...], v_ref[...]
    z_w = w0 + jnp.dot(hw_ref[...], w2_ref[...], preferred_element_type=F32)
    lw = -DECAY_SCALE * _sigmoid(z_w)
    a_sig = _sigmoid(a0 + jnp.dot(ha_ref[...], a2_ref[...], preferred_element_type=F32))
    g_ref[...] = jnp.dot(hg_ref[...], g2_ref[...], preferred_element_type=F32)
    if has_vres:
        mix = _sigmoid(v0 + jnp.dot(hv_ref[...], v2_ref[...], preferred_element_type=F32))
        v = v + (vf_ref[...] - v) * mix
    vo_ref[...] = v.astype(BF16)
    kmod = k * (1.0 + (a_sig - 1.0) * k_a)
    bonus_ref[...] = head_sum(r * kmod * r_k) * v
    kk = k * k_k
    kk = kk * lax.rsqrt(jnp.maximum(head_sum(kk * kk), 1e-24))
    b = kk * a_sig

    L = _tri_sum(tri, lw)
    l_end = L[T - 1:T, :]
    for c in range(1, tm // T):
        l_end = jnp.where(row < c * T, l_end, L[(c + 1) * T - 1:(c + 1) * T, :])
    for c in range(tm // T):
        pend_ref[c] = jnp.broadcast_to(jnp.exp(L[(c + 1) * T - 1:(c + 1) * T, :]), (8, D))
    e_pos = jnp.exp(L)
    e_neg = 1.0 / e_pos
    to_end = jnp.exp(l_end - L)
    ao_ref[...] = (-kk * jnp.exp(L - lw)).astype(BF16)
    ro_ref[...] = (r * e_pos).astype(BF16)
    bo_ref[...] = (b * e_neg).astype(BF16)
    ko_ref[...] = (kmod * e_neg).astype(BF16)
    bdo_ref[...] = (b * to_end).astype(BF16)
    kdo_ref[...] = (kmod * to_end).astype(BF16)


def rwkv_pre(rkv, hw, ha, hv, hg, w2, a2, v2, g2, vecs, v_first, tm=128):
    _, S, D = rkv.shape
    tm = min(tm, S)
    assert tm % REC_CHUNK == 0
    has_vres = v_first is not None
    row = lambda i: (i, 0)
    full = lambda i: (0, 0)
    act = pl.BlockSpec((tm, D), row)
    rkv_spec = lambda n: pl.BlockSpec((None, tm, D), lambda i: (n, i, 0))
    lora = lambda a: pl.BlockSpec((tm, a.shape[1]), row)
    wspec = lambda a: pl.BlockSpec(a.shape, full)
    if has_vres:
        in_specs = [rkv_spec(0), rkv_spec(1), rkv_spec(2), lora(hw), lora(ha), lora(hv), lora(hg),
                    wspec(w2), wspec(a2), wspec(v2), wspec(g2), wspec(vecs), rkv_spec(2)]
        args = [rkv, rkv, rkv, hw, ha, hv, hg, w2, a2, v2, g2, vecs, v_first]
    else:
        in_specs = [rkv_spec(0), rkv_spec(1), rkv_spec(2), lora(hw), lora(ha), lora(hg),
                    wspec(w2), wspec(a2), wspec(g2), wspec(vecs)]
        args = [rkv, rkv, rkv, hw, ha, hg, w2, a2, g2, vecs]
    n_chunks = S // REC_CHUNK
    pend_spec = pl.BlockSpec((tm // REC_CHUNK, 8, D), lambda i: (i, 0, 0))
    bf_out = jax.ShapeDtypeStruct((S, D), BF16)
    f32_out = jax.ShapeDtypeStruct((S, D), F32)
    return pl.pallas_call(
        functools.partial(_pre_kernel, has_vres=has_vres),
        grid=(S // tm,),
        in_specs=in_specs,
        out_specs=[act] * 7 + [pend_spec, act, act],
        out_shape=[bf_out] * 7 + [jax.ShapeDtypeStruct((n_chunks, 8, D), F32), f32_out, f32_out],
        compiler_params=_cparams("parallel"),
        name="rwkv_pre",
    )(*args)


def _rec_kernel(a_ref, r_ref, b_ref, k_ref, bd_ref, kd_ref, v_ref, pend_ref, y_ref, st_ref):
    T, W = REC_CHUNK, V7X_LANES
    D = a_ref.shape[-1]
    slabs = range(D // W)

    @pl.when(pl.program_id(0) == 0)
    def _():
        st_ref[...] = jnp.zeros_like(st_ref)

    head0 = lax.broadcasted_iota(jnp.int32, (T, W), 1) < RWKV_HEAD
    rr = lax.broadcasted_iota(jnp.int32, (2 * T, 2 * T), 0) & (T - 1)
    cc = lax.broadcasted_iota(jnp.int32, (2 * T, 2 * T), 1) & (T - 1)
    strict = cc < rr
    incl = cc <= rr

    def stack(ref, p):
        x = ref[:, p * W:(p + 1) * W].astype(F32)
        return jnp.concatenate([jnp.where(head0, x, 0.0), jnp.where(head0, 0.0, x)],
                               axis=0).astype(BF16)

    def mm(x, y):
        return jnp.dot(x.astype(BF16), y.astype(BF16), preferred_element_type=F32)

    def mm_nt(x, y):
        return lax.dot_general(x.astype(BF16), y.astype(BF16), (((1,), (1,)), ((), ())),
                               preferred_element_type=F32)

    def mm_tn(x, y):
        return lax.dot_general(x.astype(BF16), y.astype(BF16), (((0,), (0,)), ((), ())),
                               preferred_element_type=F32)

    a_s = [stack(a_ref, p) for p in slabs]
    r_s = [stack(r_ref, p) for p in slabs]
    v_s = [stack(v_ref, p) for p in slabs]
    gram = [mm_nt(jnp.concatenate([a_s[p], r_s[p]], axis=0),
                  jnp.concatenate([stack(b_ref, p), stack(k_ref, p)], axis=0)) for p in slabs]
    n_pow = [jnp.where(strict, g[:2 * T, :2 * T], 0.0).astype(BF16) for g in gram]
    a_rb = [jnp.where(incl, g[2 * T:, :2 * T], 0.0).astype(BF16) for g in gram]
    av = [mm(jnp.concatenate([jnp.where(strict, g[:2 * T, 2 * T:], 0.0),
                              jnp.where(incl, g[2 * T:, 2 * T:], 0.0)], axis=0), v_s[p])
          for p, g in zip(slabs, gram)]

    s0 = [st_ref[p] for p in slabs]
    wr = [mm_nt(jnp.concatenate([a_s[p], r_s[p]], axis=0), s0[p]) for p in slabs]

    u_s = [wr[p][:2 * T] + av[p][:2 * T] for p in slabs]
    for _ in range(5):
        z = [mm(n_pow[p], jnp.concatenate([n_pow[p], u_s[p].astype(BF16)], axis=1))
             for p in slabs]
        n_pow = [zp[:, :W].astype(BF16) for zp in z]
        u_s = [u_s[p] + z[p][:, W:] for p in slabs]
    u_s = [u_s[p] + mm(n_pow[p], u_s[p]) for p in slabs]
    for p in slabs:
        st_ref[p] = (s0[p] * pend_ref[0:1, p * W:(p + 1) * W]
                     + mm_tn(jnp.concatenate([u_s[p].astype(BF16), v_s[p]], axis=0),
                             jnp.concatenate([stack(bd_ref, p), stack(kd_ref, p)], axis=0)))
    for p in slabs:
        y_s = wr[p][2 * T:] + mm(a_rb[p], u_s[p]) + av[p][2 * T:]
        y_ref[:, p * W:(p + 1) * W] = y_s[:T] + y_s[T:]


def rwkv_recurrence(a, r, b, k, bd, kd, v, pend):
    S, D = a.shape
    T = REC_CHUNK
    act = pl.BlockSpec((T, D), lambda c: (c, 0))
    return pl.pallas_call(
        _rec_kernel,
        grid=(S // T,),
        in_specs=[act] * 7 + [pl.BlockSpec((None, 8, D), lambda c: (c, 0, 0))],
        out_specs=act,
        out_shape=jax.ShapeDtypeStruct((S, D), F32),
        scratch_shapes=[pltpu.VMEM((D // V7X_LANES, V7X_LANES, V7X_LANES), F32)],
        compiler_params=_cparams("arbitrary"),
        name="rwkv_recurrence",
    )(a, r, b, k, bd, kd, v, pend)


def _post_kernel(y_ref, bonus_ref, g_ref, lnw_ref, lnb_ref, o_ref):
    W = V7X_LANES
    ones_bd = _head_sum_matrix(W, RWKV_HEAD)
    inv_n = 1.0 / RWKV_HEAD
    for s in range(y_ref.shape[-1] // W):
        sl = slice(s * W, (s + 1) * W)
        y = y_ref[:, sl]
        d = y - _group_sum(y, ones_bd) * inv_n
        var = _group_sum(d * d, ones_bd) * inv_n
        yn = d * lax.rsqrt(var + LNX_EPS) * lnw_ref[:, sl] + lnb_ref[:, sl]
        o_ref[:, sl] = ((yn + bonus_ref[:, sl]) * g_ref[:, sl]).astype(o_ref.dtype)


def rwkv_post(y, bonus, g, lnw, lnb, tm=256):
    S, D = y.shape
    tm = min(tm, S)
    act = pl.BlockSpec((tm, D), lambda i: (i, 0))
    vec = pl.BlockSpec((1, D), lambda i: (0, 0))
    return pl.pallas_call(
        _post_kernel,
        grid=(S // tm,),
        in_specs=[act, act, act, vec, vec],
        out_specs=act,
        out_shape=jax.ShapeDtypeStruct((S, D), BF16),
        compiler_params=_cparams("parallel"),
        name="rwkv_post",
    )(y, bonus, g, lnw, lnb)


def _mlp_kernel(x_ref, g_ref, wu_ref, wd_ref, o_ref, xn_ref):
    @pl.when(pl.program_id(1) == 0)
    def _():
        x = x_ref[...]
        xn_ref[...] = _rms(x, g_ref[...]).astype(BF16)
        o_ref[...] = x

    h = jnp.dot(xn_ref[...], wu_ref[...], preferred_element_type=F32)
    h = jnp.square(jnp.maximum(h, 0.0)).astype(BF16)
    o_ref[...] += jnp.dot(h, wd_ref[...], preferred_element_type=F32)


def mlp_block(x, gain, w_up, w_down, tm=512, tf=1024):
    S, D = x.shape
    F = w_up.shape[1]
    tm, tf = min(tm, S), min(tf, F)
    return pl.pallas_call(
        _mlp_kernel,
        grid=(S // tm, F // tf),
        in_specs=[pl.BlockSpec((tm, D), lambda i, f: (i, 0)),
                  pl.BlockSpec((1, D), lambda i, f: (0, 0)),
                  pl.BlockSpec((D, tf), lambda i, f: (0, f)),
                  pl.BlockSpec((tf, D), lambda i, f: (f, 0))],
        out_specs=pl.BlockSpec((tm, D), lambda i, f: (i, 0)),
        out_shape=jax.ShapeDtypeStruct((S, D), F32),
        scratch_shapes=[pltpu.VMEM((tm, D), BF16)],
        compiler_params=_cparams("parallel", "arbitrary"),
        name="mlp",
    )(x, gain, w_up, w_down)


def _cumsum_kernel(x_ref, o_ref, carry_ref):
    @pl.when(pl.program_id(0) == 0)
    def _():
        carry_ref[...] = jnp.zeros_like(carry_ref)

    n = x_ref.shape[0]
    tri = (lax.broadcasted_iota(jnp.int32, (n, n), 1)
           <= lax.broadcasted_iota(jnp.int32, (n, n), 0)).astype(BF16)
    c = _tri_sum(tri, x_ref[...]) + carry_ref[0:1, :]
    o_ref[...] = c
    carry_ref[...] = jnp.broadcast_to(c[n - 1:n, :], carry_ref.shape)


def cumsum_rows(x, tc=256):
    S, W = x.shape
    tc = min(tc, S)
    return pl.pallas_call(
        _cumsum_kernel,
        grid=(S // tc,),
        in_specs=[pl.BlockSpec((tc, W), lambda i: (i, 0))],
        out_specs=pl.BlockSpec((tc, W), lambda i: (i, 0)),
        out_shape=jax.ShapeDtypeStruct((S, W), F32),
        scratch_shapes=[pltpu.VMEM((8, W), F32)],
        compiler_params=_cparams("arbitrary"),
        name="forget_cumsum",
    )(x)


FOX_ONES_ROWS = 16


def _fox_kernel(q_ref, k_ref, vt_ref, c_ref, o_ref, ck_ref, m_ref, acc_ref, s_ref, *, tq, tk):
    h = pl.program_id(0)
    qi = pl.program_id(1)
    nsub = tq // tk
    hd = k_ref.shape[1]
    S = k_ref.shape[0]

    @pl.when(qi == 0)
    def _():
        lane = lax.broadcasted_iota(jnp.int32, (tk, V7X_LANES), 1)
        for j in range(S // tk):
            col = jnp.sum(jnp.where(lane == h, c_ref[j * tk:(j + 1) * tk, :], 0.0),
                          axis=1, keepdims=True)
            ck_ref[j * tk:(j + 1) * tk, :] = jnp.broadcast_to(col * LOG2_E, (tk, V7X_LANES))

    m_ref[...] = jnp.full_like(m_ref, MASK_VALUE)
    acc_ref[...] = jnp.zeros_like(acc_ref)
    q = q_ref[...]
    ones_rows = jnp.ones((FOX_ONES_ROWS, tk), BF16)

    def scores(kj, slot, lo=0):
        start = pl.multiple_of(kj * tk, tk)
        s_ref[slot, :, lo:] = lax.dot_general(
            k_ref[pl.ds(start, tk), :], q[lo:], (((1,), (1,)), ((), ())),
            preferred_element_type=F32)

    def update(kj, slot, lo=0, hi=tq, diagonal=False):
        start = pl.multiple_of(kj * tk, tk)
        ck = ck_ref[pl.ds(start, tk), :]
        s = s_ref[slot, :, lo:hi] - jnp.concatenate([ck] * ((hi - lo) // V7X_LANES), axis=1)
        if diagonal:
            key = lax.broadcasted_iota(jnp.int32, (tk, tk), 0)
            qry = lax.broadcasted_iota(jnp.int32, (tk, tk), 1)
            s = jnp.where(qry >= key, s, MASK_VALUE)
        m_prev = m_ref[0:1, lo:hi]
        m_new = jnp.maximum(m_prev, jnp.max(s, axis=0, keepdims=True))
        alpha = jnp.exp2(m_prev - m_new)
        p = jnp.exp2(s - m_new).astype(BF16)
        vt = jnp.concatenate([vt_ref[kj], ones_rows], axis=0)
        acc_ref[:, lo:hi] = (alpha * acc_ref[:, lo:hi]
                             + jnp.dot(vt, p, preferred_element_type=F32))
        m_ref[0:1, lo:hi] = m_new

    assert nsub == 2
    scores(0, 0)

    def full_blocks(i, carry):
        scores(2 * i + 1, 1)
        update(2 * i, 0)
        scores(2 * i + 2, 0)
        update(2 * i + 1, 1)
        return carry

    lax.fori_loop(0, qi, full_blocks, 0)
    scores(2 * qi + 1, 1, lo=tk)
    update(2 * qi, 0, 0, tk, diagonal=True)
    update(2 * qi, 0, tk, tq)
    update(2 * qi + 1, 1, tk, tq, diagonal=True)
    out_t = acc_ref[0:hd, :] / acc_ref[hd:hd + 1, :]
    o_ref[...] = out_t.T.astype(o_ref.dtype)


def fox_attention(q, k, vt, c, n_heads, tq=1024, tk=512):
    S = q.shape[0]
    hd = FOX_HEAD
    tq = min(tq, S)
    assert tq % tk == 0 and vt.shape[-1] == tk
    return pl.pallas_call(
        functools.partial(_fox_kernel, tq=tq, tk=tk),
        grid=(n_heads, S // tq),
        in_specs=[pl.BlockSpec((tq, hd), lambda h, i: (i, h)),
                  pl.BlockSpec((S, hd), lambda h, i: (0, h)),
                  pl.BlockSpec((None, S // tk, hd, tk), lambda h, i: (h, 0, 0, 0)),
                  pl.BlockSpec((S, V7X_LANES), lambda h, i: (0, 0))],
        out_specs=pl.BlockSpec((tq, hd), lambda h, i: (i, h)),
        out_shape=jax.ShapeDtypeStruct((S, n_heads * hd), BF16),
        scratch_shapes=[pltpu.VMEM((S, V7X_LANES), F32),
                        pltpu.VMEM((8, tq), F32),
                        pltpu.VMEM((hd + FOX_ONES_ROWS, tq), F32),
                        pltpu.VMEM((2, tk, tq), F32)],
        compiler_params=_cparams("parallel", "arbitrary"),
        name="fox_attention",
    )(q, k, vt, c)


def _pad_cols(w, n):
    return jnp.pad(w, ((0, 0), (0, n - w.shape[1])))


def _pad_rows(w, n):
    return jnp.pad(w, ((0, n - w.shape[0]), (0, 0)))


def kernel(x, mix_norm, ffn_norm, final_norm, rwkv_x_mix, rwkv_w_rkv, rwkv_w0, rwkv_w1, rwkv_w2, rwkv_a0, rwkv_a1, rwkv_a2, rwkv_v0, rwkv_v1, rwkv_v2, rwkv_g1, rwkv_g2, rwkv_k_k, rwkv_k_a, rwkv_r_k, rwkv_lnx_w, rwkv_lnx_b, rwkv_w_o, kv_norm, w_kvf, b_f, fox_w_q, fox_w_o, mlp_w_up, mlp_w_down):
    B, S, D = x.shape
    assert B == 1
    n_a = rwkv_x_mix.shape[0]
    n_b = fox_w_q.shape[0]
    n_fox_heads = D // FOX_HEAD
    lora_pad = V7X_LANES
    fox_tk = min(512, S)
    xs = x.reshape(S, D)
    bf = lambda w: w.astype(BF16)

    v_first = None
    for i in range(n_a):
        mixed = rwkv_mix(xs, mix_norm[i:i + 1], rwkv_x_mix[i])
        rkv = matmul(mixed, rwkv_w_rkv, w_idx=(i,), name="rkv_proj")
        hw = matmul(mixed, bf(_pad_cols(rwkv_w1[i], lora_pad)), x_idx=3, act=jnp.tanh,
                    out_dtype=BF16, name="lora_w")
        ha = matmul(mixed, bf(_pad_cols(rwkv_a1[i], lora_pad)), x_idx=4, out_dtype=BF16,
                    name="lora_a")
        hg = matmul(mixed, bf(rwkv_g1[i]), x_idx=5, act=_sigmoid, out_dtype=BF16, name="lora_g")
        w2 = bf(_pad_rows(rwkv_w2[i], lora_pad))
        a2 = bf(_pad_rows(rwkv_a2[i], lora_pad))
        g2 = bf(rwkv_g2[i])
        zeros = jnp.zeros((D,), F32)
        r_k = rwkv_r_k[i].reshape(D)
        if i == 0:
            vecs = jnp.stack([rwkv_w0[i], rwkv_a0[i], zeros, rwkv_k_k[i], rwkv_k_a[i], r_k,
                              zeros, zeros])
            pre = rwkv_pre(rkv, hw, ha, None, hg, w2, a2, None, g2, vecs, None)
            v_first = rkv
        else:
            vecs = jnp.stack([rwkv_w0[i], rwkv_a0[i], rwkv_v0[i - 1], rwkv_k_k[i], rwkv_k_a[i],
                              r_k, zeros, zeros])
            hv = matmul(mixed, bf(_pad_cols(rwkv_v1[i - 1], lora_pad)), x_idx=2, out_dtype=BF16,
                        name="lora_v")
            v2 = bf(_pad_rows(rwkv_v2[i - 1], lora_pad))
            pre = rwkv_pre(rkv, hw, ha, hv, hg, w2, a2, v2, g2, vecs, v_first)
        *rec_in, bonus, g = pre
        y = rwkv_recurrence(*rec_in)
        y = rwkv_post(y, bonus, g, rwkv_lnx_w[i:i + 1], rwkv_lnx_b[i:i + 1])
        xs = matmul(y, rwkv_w_o, w_idx=(i,), res=xs, name="rwkv_out")
        xs = mlp_block(xs, ffn_norm[i:i + 1], bf(mlp_w_up[i]), bf(mlp_w_down[i]))

    kv = vt = c = None
    for j in range(n_b):
        layer = n_a + j
        if j == 0:
            normed = rms_norm_multi(xs, jnp.stack([mix_norm[layer], kv_norm]), BF16)
            kv = matmul(normed, w_kvf, x_idx=1, n_out=2 * D, out_dtype=BF16, name="kv_proj")
            b_pad = jnp.pad(b_f, (0, V7X_LANES - n_fox_heads)).reshape(1, V7X_LANES)
            log_f = matmul(normed, bf(_pad_cols(w_kvf[:, 2 * D:], V7X_LANES)), x_idx=1,
                           bias=b_pad, act=_log_sigmoid, name="forget_proj")
            c = cumsum_rows(log_f)
            vt = kv[:, D:].reshape(S // fox_tk, fox_tk, n_fox_heads, FOX_HEAD).transpose(2, 0, 3, 1)
        else:
            normed = rms_norm_multi(xs, mix_norm[layer:layer + 1], BF16)
        q = matmul(normed, fox_w_q, w_idx=(j,), x_idx=0, scale=LOG2_E * FOX_HEAD ** -0.5,
                   out_dtype=BF16, name="q_proj")
        o = fox_attention(q, kv, vt, c, n_fox_heads, tk=fox_tk)
        xs = matmul(o, fox_w_o, w_idx=(j,), res=xs, name="fox_out")
        xs = mlp_block(xs, ffn_norm[layer:layer + 1], bf(mlp_w_up[layer]), bf(mlp_w_down[layer]))

    out = rms_norm_multi(xs, final_norm.reshape(1, D), F32)
    return out.reshape(B, S, D)
```

```python
import functools
import math

import jax
import jax.numpy as jnp
from jax import lax
from jax.experimental import pallas as pl
from jax.experimental.pallas import tpu as pltpu

F32 = jnp.float32
BF16 = jnp.bfloat16

RWKV_HEAD = 64
FOX_HEAD = 128
RMS_EPS = 1e-6
LNX_EPS = 1e-5 * RWKV_HEAD
V7X_LANES = 128
REC_CHUNK = 64
V7X_VMEM_LIMIT = 56 * 1024 * 1024
MASK_VALUE = -1e30
LOG2_E = math.log2(math.e)
DECAY_SCALE = math.exp(-0.5)


def _cparams(*sem):
    return pltpu.CompilerParams(dimension_semantics=sem, vmem_limit_bytes=V7X_VMEM_LIMIT)


def _sigmoid(z):
    return 1.0 / (1.0 + jnp.exp(-z))


def _log_sigmoid(z):
    return jnp.minimum(z, 0.0) - jnp.log(1.0 + jnp.exp(-jnp.abs(z)))


def _rms(x, gain):
    return x * lax.rsqrt(jnp.mean(x * x, axis=-1, keepdims=True) + RMS_EPS) * gain


def _head_sum_matrix(n, head):
    r = lax.broadcasted_iota(jnp.int32, (n, n), 0) // head
    c = lax.broadcasted_iota(jnp.int32, (n, n), 1) // head
    return (r == c).astype(BF16)


def _group_sum(x, ones_bd):
    return jnp.dot(x.astype(BF16), ones_bd, preferred_element_type=F32)


def _tri_sum(tri, x):
    hi = x.astype(BF16)
    r1 = x - hi.astype(F32)
    mid = r1.astype(BF16)
    lo = (r1 - mid.astype(F32)).astype(BF16)
    return (jnp.dot(tri, hi, preferred_element_type=F32)
            + jnp.dot(tri, mid, preferred_element_type=F32)
            + jnp.dot(tri, lo, preferred_element_type=F32))


def _rms_kernel(x_ref, g_ref, o_ref):
    x = x_ref[...]
    y = x * lax.rsqrt(jnp.mean(x * x, axis=-1, keepdims=True) + RMS_EPS)
    for n in range(o_ref.shape[0]):
        o_ref[n] = (y * g_ref[n:n + 1, :]).astype(o_ref.dtype)


def rms_norm_multi(x, gains, out_dtype, tm=512):
    S, D = x.shape
    G = gains.shape[0]
    tm = min(tm, S)
    return pl.pallas_call(
        _rms_kernel,
        grid=(S // tm,),
        in_specs=[pl.BlockSpec((tm, D), lambda i: (i, 0)),
                  pl.BlockSpec((G, D), lambda i: (0, 0))],
        out_specs=pl.BlockSpec((G, tm, D), lambda i: (0, i, 0)),
        out_shape=jax.ShapeDtypeStruct((G, S, D), out_dtype),
        compiler_params=_cparams("parallel"),
        name="rms_norm",
    )(x, gains)


def _mix_kernel(x_ref, halo_ref, g_ref, mix_ref, w1_ref, a1_ref, g1_ref, *rest, has_vres):
    if has_vres:
        v1_ref, o_ref, hw_ref, ha_ref, hg_ref, hv_ref = rest
    else:
        o_ref, hw_ref, ha_ref, hg_ref = rest
    i = pl.program_id(0)
    gain = g_ref[...]
    h = _rms(x_ref[...], gain)
    hh = _rms(halo_ref[...], gain)
    last = jnp.where(i == 0, 0.0, hh[7:8, :])
    row = lax.broadcasted_iota(jnp.int32, h.shape, 0)
    prev = jnp.where(row == 0, last, pltpu.roll(h, 1, 0))
    d = prev - h
    mixed = lambda n: (h + d * mix_ref[n:n + 1, :]).astype(BF16)
    lora = lambda n, w_ref: jnp.dot(mixed(n), w_ref[...], preferred_element_type=F32)
    xv = mixed(2)
    o_ref[0] = mixed(0)
    o_ref[1] = mixed(1)
    o_ref[2] = xv
    hw_ref[...] = jnp.tanh(lora(3, w1_ref)).astype(BF16)
    ha_ref[...] = lora(4, a1_ref).astype(BF16)
    hg_ref[...] = _sigmoid(lora(5, g1_ref)).astype(BF16)
    if has_vres:
        hv_ref[...] = jnp.dot(xv, v1_ref[...], preferred_element_type=F32).astype(BF16)


def rwkv_mix(x, gain, x_mix, w1, a1, g1, v1, tm=256):
    S, D = x.shape
    tm = min(tm, S)
    hb = tm // 8
    has_vres = v1 is not None
    row = lambda i: (i, 0)
    full = lambda i: (0, 0)
    loras = [w1, a1, g1] + ([v1] if has_vres else [])
    return pl.pallas_call(
        functools.partial(_mix_kernel, has_vres=has_vres),
        grid=(S // tm,),
        in_specs=[pl.BlockSpec((tm, D), row),
                  pl.BlockSpec((8, D), lambda i: (jnp.maximum(i * hb - 1, 0), 0)),
                  pl.BlockSpec((1, D), full),
                  pl.BlockSpec(x_mix.shape, full)]
                 + [pl.BlockSpec(w.shape, full) for w in loras],
        out_specs=[pl.BlockSpec((3, tm, D), lambda i: (0, i, 0))]
                  + [pl.BlockSpec((tm, w.shape[1]), row) for w in loras],
        out_shape=[jax.ShapeDtypeStruct((3, S, D), BF16)]
                  + [jax.ShapeDtypeStruct((S, w.shape[1]), BF16) for w in loras],
        compiler_params=_cparams("parallel"),
        name="rwkv_mix",
    )(x, x, gain, x_mix, *loras)


def _mm_kernel(x_ref, w_ref, *rest, act, scale, has_bias, has_res, cast_w, row_axis):
    if cast_w:
        o_ref, wbf_ref = rest[-2], rest[-1]

        @pl.when(pl.program_id(row_axis) == 0)
        def _():
            wbf_ref[...] = w_ref[...].astype(BF16)
        w = wbf_ref[...]
    else:
        o_ref = rest[-1]
        w = w_ref[...]
    acc = jnp.dot(x_ref[...], w, preferred_element_type=F32)
    if scale != 1.0:
        acc = acc * scale
    if has_bias:
        acc = acc + rest[0][...]
    if act is not None:
        acc = act(acc)
    if has_res:
        acc = acc + rest[1 if has_bias else 0][...]
    o_ref[...] = acc.astype(o_ref.dtype)


def matmul(x, w, *, x_idx=None, w_idx=(), n_out=None, act=None, scale=1.0, bias=None, res=None,
           out_dtype=F32, tm=1024, tn=1024, name="matmul"):
    M, K = x.shape[-2:]
    N = w.shape[-1] if n_out is None else n_out
    batched = w.ndim - len(w_idx) == 3
    tm, tn = min(tm, M), min(tn, N)
    cast_w = w.dtype != BF16
    lead = (None,) * len(w_idx)
    if batched:
        nb = w.shape[len(w_idx)]
        grid = (nb, N // tn, M // tm)
        x_spec = pl.BlockSpec((None, tm, K), lambda b, j, i: (b, i, 0))
        w_spec = pl.BlockSpec(lead + (None, K, tn), lambda b, j, i: w_idx + (b, 0, j))
        o_spec = pl.BlockSpec((None, tm, tn), lambda b, j, i: (b, i, j))
        out_shape = jax.ShapeDtypeStruct((nb, M, N), out_dtype)
        sem = ("parallel", "parallel", "arbitrary")
    else:
        grid = (N // tn, M // tm)
        if x_idx is None:
            x_spec = pl.BlockSpec((tm, K), lambda j, i: (i, 0))
        else:
            x_spec = pl.BlockSpec((None, tm, K), lambda j, i: (x_idx, i, 0))
        w_spec = pl.BlockSpec(lead + (K, tn), lambda j, i: w_idx + (0, j))
        o_spec = pl.BlockSpec((tm, tn), lambda j, i: (i, j))
        out_shape = jax.ShapeDtypeStruct((M, N), out_dtype)
        sem = ("parallel", "arbitrary")
    kern = functools.partial(_mm_kernel, act=act, scale=scale, has_bias=bias is not None,
                             has_res=res is not None, cast_w=cast_w, row_axis=len(grid) - 1)
    in_specs = [x_spec, w_spec]
    args = [x, w]
    if bias is not None:
        assert not batched
        in_specs.append(pl.BlockSpec((1, tn), lambda j, i: (0, j)))
        args.append(bias)
    if res is not None:
        assert not batched
        in_specs.append(o_spec)
        args.append(res)
    return pl.pallas_call(
        kern, grid=grid, in_specs=in_specs, out_specs=o_spec, out_shape=out_shape,
        scratch_shapes=[pltpu.VMEM((K, tn), BF16)] if cast_w else [],
        compiler_params=_cparams(*sem), name=name,
    )(*args)


def _pre_kernel(*refs, has_vres):
    if has_vres:
        (r_ref, k_ref, v_ref, hw_ref, ha_ref, hv_ref, w2_ref, a2_ref, v2_ref,
         vec_ref, vf_ref, *outs) = refs
    else:
        (r_ref, k_ref, v_ref, hw_ref, ha_ref, w2_ref, a2_ref, vec_ref, *outs) = refs
    ao_ref, ro_ref, bo_ref, ko_ref, bdo_ref, kdo_ref, vo_ref, pend_ref, bonus_ref = outs
    tm, D = k_ref.shape
    T, W = REC_CHUNK, V7X_LANES
    ri = lax.broadcasted_iota(jnp.int32, (tm, tm), 0)
    ci = lax.broadcasted_iota(jnp.int32, (tm, tm), 1)
    tri = jnp.logical_and(ci <= ri, ci // T == ri // T).astype(BF16)
    row = lax.broadcasted_iota(jnp.int32, (tm, D), 0)
    ones_bd = _head_sum_matrix(W, RWKV_HEAD)
    w0, a0, v0, k_k, k_a, r_k = (vec_ref[n:n + 1, :] for n in range(6))

    def head_sum(x):
        return jnp.concatenate([_group_sum(x[:, s * W:(s + 1) * W], ones_bd)
                                for s in range(D // W)], axis=1)

    r, k, v = r_ref[...], k_ref[...], v_ref[...]
    z_w = w0 + jnp.dot(hw_ref[...], w2_ref[...], preferred_element_type=F32)
    lw = -DECAY_SCALE * _sigmoid(z_w)
    a_sig = _sigmoid(a0 + jnp.dot(ha_ref[...], a2_ref[...], preferred_element_type=F32))
    if has_vres:
        mix = _sigmoid(v0 + jnp.dot(hv_ref[...], v2_ref[...], preferred_element_type=F32))
        v = v + (vf_ref[...] - v) * mix
    vo_ref[...] = v.astype(BF16)
    kmod = k * (1.0 + (a_sig - 1.0) * k_a)
    bonus_ref[...] = head_sum(r * kmod * r_k) * v
    kk = k * k_k
    kk = kk * lax.rsqrt(jnp.maximum(head_sum(kk * kk), 1e-24))
    b = kk * a_sig

    L = _tri_sum(tri, lw)
    l_end = L[T - 1:T, :]
    for c in range(1, tm // T):
        l_end = jnp.where(row < c * T, l_end, L[(c + 1) * T - 1:(c + 1) * T, :])
    for c in range(tm // T):
        pend_ref[c] = jnp.broadcast_to(jnp.exp(L[(c + 1) * T - 1:(c + 1) * T, :]), (8, D))
    e_pos = jnp.exp(L)
    e_neg = 1.0 / e_pos
    to_end = jnp.exp(l_end - L)
    ao_ref[...] = (-kk * jnp.exp(L - lw)).astype(BF16)
    ro_ref[...] = (r * e_pos).astype(BF16)
    bo_ref[...] = (b * e_neg).astype(BF16)
    ko_ref[...] = (kmod * e_neg).astype(BF16)
    bdo_ref[...] = (b * to_end).astype(BF16)
    kdo_ref[...] = (kmod * to_end).astype(BF16)


def rwkv_pre(rkv, hw, ha, hv, w2, a2, v2, vecs, v_first, tm=128):
    _, S, D = rkv.shape
    tm = min(tm, S)
    assert tm % REC_CHUNK == 0
    has_vres = v_first is not None
    row = lambda i: (i, 0)
    full = lambda i: (0, 0)
    act = pl.BlockSpec((tm, D), row)
    rkv_spec = lambda n: pl.BlockSpec((None, tm, D), lambda i: (n, i, 0))
    lora = lambda a: pl.BlockSpec((tm, a.shape[1]), row)
    wspec = lambda a: pl.BlockSpec(a.shape, full)
    if has_vres:
        in_specs = [rkv_spec(0), rkv_spec(1), rkv_spec(2), lora(hw), lora(ha), lora(hv),
                    wspec(w2), wspec(a2), wspec(v2), wspec(vecs), rkv_spec(2)]
        args = [rkv, rkv, rkv, hw, ha, hv, w2, a2, v2, vecs, v_first]
    else:
        in_specs = [rkv_spec(0), rkv_spec(1), rkv_spec(2), lora(hw), lora(ha),
                    wspec(w2), wspec(a2), wspec(vecs)]
        args = [rkv, rkv, rkv, hw, ha, w2, a2, vecs]
    n_chunks = S // REC_CHUNK
    pend_spec = pl.BlockSpec((tm // REC_CHUNK, 8, D), lambda i: (i, 0, 0))
    bf_out = jax.ShapeDtypeStruct((S, D), BF16)
    f32_out = jax.ShapeDtypeStruct((S, D), F32)
    return pl.pallas_call(
        functools.partial(_pre_kernel, has_vres=has_vres),
        grid=(S // tm,),
        in_specs=in_specs,
        out_specs=[act] * 7 + [pend_spec, act],
        out_shape=[bf_out] * 7 + [jax.ShapeDtypeStruct((n_chunks, 8, D), F32), f32_out],
        compiler_params=_cparams("parallel"),
        name="rwkv_pre",
    )(*args)


def _rec_kernel(a_ref, r_ref, b_ref, k_ref, bd_ref, kd_ref, v_ref, pend_ref, y_ref, st_ref):
    T, W = REC_CHUNK, V7X_LANES
    D = a_ref.shape[-1]
    slabs = range(D // W)

    @pl.when(pl.program_id(0) == 0)
    def _():
        st_ref[...] = jnp.zeros_like(st_ref)

    head0 = lax.broadcasted_iota(jnp.int32, (T, W), 1) < RWKV_HEAD
    rr = lax.broadcasted_iota(jnp.int32, (2 * T, 2 * T), 0) & (T - 1)
    cc = lax.broadcasted_iota(jnp.int32, (2 * T, 2 * T), 1) & (T - 1)
    strict = cc < rr
    incl = cc <= rr

    def stack(ref, p):
        x = ref[:, p * W:(p + 1) * W].astype(F32)
        return jnp.concatenate([jnp.where(head0, x, 0.0), jnp.where(head0, 0.0, x)],
                               axis=0).astype(BF16)

    def mm(x, y):
        return jnp.dot(x.astype(BF16), y.astype(BF16), preferred_element_type=F32)

    def mm_nt(x, y):
        return lax.dot_general(x.astype(BF16), y.astype(BF16), (((1,), (1,)), ((), ())),
                               preferred_element_type=F32)

    def mm_tn(x, y):
        return lax.dot_general(x.astype(BF16), y.astype(BF16), (((0,), (0,)), ((), ())),
                               preferred_element_type=F32)

    a_s = [stack(a_ref, p) for p in slabs]
    r_s = [stack(r_ref, p) for p in slabs]
    v_s = [stack(v_ref, p) for p in slabs]
    gram = [mm_nt(jnp.concatenate([a_s[p], r_s[p]], axis=0),
                  jnp.concatenate([stack(b_ref, p), stack(k_ref, p)], axis=0)) for p in slabs]
    n_pow = [jnp.where(strict, g[:2 * T, :2 * T], 0.0).astype(BF16) for g in gram]
    a_rb = [jnp.where(incl, g[2 * T:, :2 * T], 0.0).astype(BF16) for g in gram]
    av = [mm(jnp.concatenate([jnp.where(strict, g[:2 * T, 2 * T:], 0.0),
                              jnp.where(incl, g[2 * T:, 2 * T:], 0.0)], axis=0), v_s[p])
          for p, g in zip(slabs, gram)]

    s0 = [st_ref[p] for p in slabs]
    wr = [mm_nt(jnp.concatenate([a_s[p], r_s[p]], axis=0), s0[p]) for p in slabs]

    u_s = [wr[p][:2 * T] + av[p][:2 * T] for p in slabs]
    for _ in range(5):
        z = [mm(n_pow[p], jnp.concatenate([n_pow[p], u_s[p].astype(BF16)], axis=1))
             for p in slabs]
        n_pow = [zp[:, :W].astype(BF16) for zp in z]
        u_s = [u_s[p] + z[p][:, W:] for p in slabs]
    u_s = [u_s[p] + mm(n_pow[p], u_s[p]) for p in slabs]
    for p in slabs:
        st_ref[p] = (s0[p] * pend_ref[0:1, p * W:(p + 1) * W]
                     + mm_tn(jnp.concatenate([u_s[p].astype(BF16), v_s[p]], axis=0),
                             jnp.concatenate([stack(bd_ref, p), stack(kd_ref, p)], axis=0)))
    for p in slabs:
        y_s = wr[p][2 * T:] + mm(a_rb[p], u_s[p]) + av[p][2 * T:]
        y_ref[:, p * W:(p + 1) * W] = y_s[:T] + y_s[T:]


def rwkv_recurrence(a, r, b, k, bd, kd, v, pend):
    S, D = a.shape
    T = REC_CHUNK
    act = pl.BlockSpec((T, D), lambda c: (c, 0))
    return pl.pallas_call(
        _rec_kernel,
        grid=(S // T,),
        in_specs=[act] * 7 + [pl.BlockSpec((None, 8, D), lambda c: (c, 0, 0))],
        out_specs=act,
        out_shape=jax.ShapeDtypeStruct((S, D), F32),
        scratch_shapes=[pltpu.VMEM((D // V7X_LANES, V7X_LANES, V7X_LANES), F32)],
        compiler_params=_cparams("arbitrary"),
        name="rwkv_recurrence",
    )(a, r, b, k, bd, kd, v, pend)


def _post_kernel(y_ref, bonus_ref, hg_ref, g2_ref, lnw_ref, lnb_ref, o_ref):
    W = V7X_LANES
    ones_bd = _head_sum_matrix(W, RWKV_HEAD)
    inv_n = 1.0 / RWKV_HEAD

    def head_sum(x):
        return jnp.concatenate([_group_sum(x[:, s * W:(s + 1) * W], ones_bd)
                                for s in range(x.shape[-1] // W)], axis=1)

    y = y_ref[...]
    d = y - head_sum(y) * inv_n
    var = head_sum(d * d) * inv_n
    yn = d * lax.rsqrt(var + LNX_EPS) * lnw_ref[...] + lnb_ref[...]
    gate = jnp.dot(hg_ref[...], g2_ref[...], preferred_element_type=F32)
    o_ref[...] = ((yn + bonus_ref[...]) * gate).astype(o_ref.dtype)


def rwkv_post(y, bonus, hg, g2, lnw, lnb, tm=256):
    S, D = y.shape
    tm = min(tm, S)
    act = pl.BlockSpec((tm, D), lambda i: (i, 0))
    vec = pl.BlockSpec((1, D), lambda i: (0, 0))
    return pl.pallas_call(
        _post_kernel,
        grid=(S // tm,),
        in_specs=[act, act, pl.BlockSpec((tm, hg.shape[1]), lambda i: (i, 0)),
                  pl.BlockSpec(g2.shape, lambda i: (0, 0)), vec, vec],
        out_specs=act,
        out_shape=jax.ShapeDtypeStruct((S, D), BF16),
        compiler_params=_cparams("parallel"),
        name="rwkv_post",
    )(y, bonus, hg, g2, lnw, lnb)


def _mlp_kernel(x_ref, g_ref, wu_ref, wd_ref, o_ref, xn_ref):
    @pl.when(pl.program_id(1) == 0)
    def _():
        x = x_ref[...]
        xn_ref[...] = _rms(x, g_ref[...]).astype(BF16)
        o_ref[...] = x

    h = jnp.dot(xn_ref[...], wu_ref[...], preferred_element_type=F32)
    h = jnp.square(jnp.maximum(h, 0.0)).astype(BF16)
    o_ref[...] += jnp.dot(h, wd_ref[...], preferred_element_type=F32)


def mlp_block(x, gain, w_up, w_down, layer, tm=512, tf=1024):
    S, D = x.shape
    F = w_up.shape[-1]
    tm, tf = min(tm, S), min(tf, F)
    return pl.pallas_call(
        _mlp_kernel,
        grid=(S // tm, F // tf),
        in_specs=[pl.BlockSpec((tm, D), lambda i, f: (i, 0)),
                  pl.BlockSpec((1, D), lambda i, f: (0, 0)),
                  pl.BlockSpec((None, D, tf), lambda i, f: (layer, 0, f)),
                  pl.BlockSpec((None, tf, D), lambda i, f: (layer, f, 0))],
        out_specs=pl.BlockSpec((tm, D), lambda i, f: (i, 0)),
        out_shape=jax.ShapeDtypeStruct((S, D), F32),
        scratch_shapes=[pltpu.VMEM((tm, D), BF16)],
        compiler_params=_cparams("parallel", "arbitrary"),
        name="mlp",
    )(x, gain, w_up, w_down)


def _cumsum_kernel(x_ref, o_ref, carry_ref):
    @pl.when(pl.program_id(0) == 0)
    def _():
        carry_ref[...] = jnp.zeros_like(carry_ref)

    n = x_ref.shape[0]
    tri = (lax.broadcasted_iota(jnp.int32, (n, n), 1)
           <= lax.broadcasted_iota(jnp.int32, (n, n), 0)).astype(BF16)
    c = _tri_sum(tri, x_ref[...]) + carry_ref[0:1, :]
    o_ref[...] = c
    carry_ref[...] = jnp.broadcast_to(c[n - 1:n, :], carry_ref.shape)


def cumsum_rows(x, tc=256):
    S, W = x.shape
    tc = min(tc, S)
    return pl.pallas_call(
        _cumsum_kernel,
        grid=(S // tc,),
        in_specs=[pl.BlockSpec((tc, W), lambda i: (i, 0))],
        out_specs=pl.BlockSpec((tc, W), lambda i: (i, 0)),
        out_shape=jax.ShapeDtypeStruct((S, W), F32),
        scratch_shapes=[pltpu.VMEM((8, W), F32)],
        compiler_params=_cparams("arbitrary"),
        name="forget_cumsum",
    )(x)


FOX_ONES_ROWS = 16


def _fox_kernel(q_ref, k_ref, vt_ref, c_ref, o_ref, ck_ref, m_ref, acc_ref, s_ref, *, tq, tk):
    h = pl.program_id(0)
    qi = pl.program_id(1)
    nsub = tq // tk
    hd = k_ref.shape[1]
    S = k_ref.shape[0]

    @pl.when(qi == 0)
    def _():
        lane = lax.broadcasted_iota(jnp.int32, (tk, V7X_LANES), 1)
        for j in range(S // tk):
            col = jnp.sum(jnp.where(lane == h, c_ref[j * tk:(j + 1) * tk, :], 0.0),
                          axis=1, keepdims=True)
            ck_ref[j * tk:(j + 1) * tk, :] = jnp.broadcast_to(col * LOG2_E, (tk, V7X_LANES))

    m_ref[...] = jnp.full_like(m_ref, MASK_VALUE)
    acc_ref[...] = jnp.zeros_like(acc_ref)
    q = q_ref[...]
    ones_rows = jnp.ones((FOX_ONES_ROWS, tk), BF16)

    def scores(kj, slot, lo=0):
        start = pl.multiple_of(kj * tk, tk)
        s_ref[slot, :, lo:] = lax.dot_general(
            k_ref[pl.ds(start, tk), :], q[lo:], (((1,), (1,)), ((), ())),
            preferred_element_type=F32)

    def update(kj, slot, lo=0, hi=tq, diagonal=False):
        start = pl.multiple_of(kj * tk, tk)
        ck = ck_ref[pl.ds(start, tk), :]
        s = s_ref[slot, :, lo:hi] - jnp.concatenate([ck] * ((hi - lo) // V7X_LANES), axis=1)
        if diagonal:
            key = lax.broadcasted_iota(jnp.int32, (tk, tk), 0)
            qry = lax.broadcasted_iota(jnp.int32, (tk, tk), 1)
            s = jnp.where(qry >= key, s, MASK_VALUE)
        m_prev = m_ref[0:1, lo:hi]
        m_new = jnp.maximum(m_prev, jnp.max(s, axis=0, keepdims=True))
        alpha = jnp.exp2(m_prev - m_new)
        p = jnp.exp2(s - m_new).astype(BF16)
        vt = jnp.concatenate([vt_ref[kj], ones_rows], axis=0)
        acc_ref[:, lo:hi] = (alpha * acc_ref[:, lo:hi]
                             + jnp.dot(vt, p, preferred_element_type=F32))
        m_ref[0:1, lo:hi] = m_new

    assert nsub == 2
    scores(0, 0)

    def full_blocks(i, carry):
        scores(2 * i + 1, 1)
        update(2 * i, 0)
        scores(2 * i + 2, 0)
        update(2 * i + 1, 1)
        return carry

    lax.fori_loop(0, qi, full_blocks, 0)
    scores(2 * qi + 1, 1, lo=tk)
    update(2 * qi, 0, 0, tk, diagonal=True)
    update(2 * qi, 0, tk, tq)
    update(2 * qi + 1, 1, tk, tq, diagonal=True)
    out_t = acc_ref[0:hd, :] / acc_ref[hd:hd + 1, :]
    o_ref[...] = out_t.T.astype(o_ref.dtype)


def fox_attention(q, k, vt, c, n_heads, tq=1024, tk=512):
    S = q.shape[0]
    hd = FOX_HEAD
    tq = min(tq, S)
    assert tq % tk == 0 and vt.shape[-1] == tk
    return pl.pallas_call(
        functools.partial(_fox_kernel, tq=tq, tk=tk),
        grid=(n_heads, S // tq),
        in_specs=[pl.BlockSpec((tq, hd), lambda h, i: (i, h)),
                  pl.BlockSpec((S, hd), lambda h, i: (0, h)),
                  pl.BlockSpec((None, S // tk, hd, tk), lambda h, i: (h, 0, 0, 0)),
                  pl.BlockSpec((S, V7X_LANES), lambda h, i: (0, 0))],
        out_specs=pl.BlockSpec((tq, hd), lambda h, i: (i, h)),
        out_shape=jax.ShapeDtypeStruct((S, n_heads * hd), BF16),
        scratch_shapes=[pltpu.VMEM((S, V7X_LANES), F32),
                        pltpu.VMEM((8, tq), F32),
                        pltpu.VMEM((hd + FOX_ONES_ROWS, tq), F32),
                        pltpu.VMEM((2, tk, tq), F32)],
        compiler_params=_cparams("parallel", "arbitrary"),
        name="fox_attention",
    )(q, k, vt, c)


def _pad_cols(w, n):
    return jnp.pad(w, ((0, 0), (0, n - w.shape[1])))


def _pad_rows(w, n):
    return jnp.pad(w, ((0, n - w.shape[0]), (0, 0)))


def kernel(x, mix_norm, ffn_norm, final_norm, rwkv_x_mix, rwkv_w_rkv, rwkv_w0, rwkv_w1, rwkv_w2, rwkv_a0, rwkv_a1, rwkv_a2, rwkv_v0, rwkv_v1, rwkv_v2, rwkv_g1, rwkv_g2, rwkv_k_k, rwkv_k_a, rwkv_r_k, rwkv_lnx_w, rwkv_lnx_b, rwkv_w_o, kv_norm, w_kvf, b_f, fox_w_q, fox_w_o, mlp_w_up, mlp_w_down):
    B, S, D = x.shape
    assert B == 1
    n_a = rwkv_x_mix.shape[0]
    n_b = fox_w_q.shape[0]
    n_fox_heads = D // FOX_HEAD
    lora_pad = V7X_LANES
    fox_tk = min(512, S)
    xs = x.reshape(S, D)
    bf = lambda w: w.astype(BF16)
    w_up_bf, w_down_bf = bf(mlp_w_up), bf(mlp_w_down)

    v_first = None
    for i in range(n_a):
        w1 = bf(_pad_cols(rwkv_w1[i], lora_pad))
        a1 = bf(_pad_cols(rwkv_a1[i], lora_pad))
        v1 = bf(_pad_cols(rwkv_v1[i - 1], lora_pad)) if i > 0 else None
        mixed, hw, ha, hg, *hv = rwkv_mix(xs, mix_norm[i:i + 1], rwkv_x_mix[i], w1, a1,
                                          bf(rwkv_g1[i]), v1)
        rkv = matmul(mixed, rwkv_w_rkv, w_idx=(i,), name="rkv_proj")
        w2 = bf(_pad_rows(rwkv_w2[i], lora_pad))
        a2 = bf(_pad_rows(rwkv_a2[i], lora_pad))
        zeros = jnp.zeros((D,), F32)
        r_k = rwkv_r_k[i].reshape(D)
        if i == 0:
            vecs = jnp.stack([rwkv_w0[i], rwkv_a0[i], zeros, rwkv_k_k[i], rwkv_k_a[i], r_k,
                              zeros, zeros])
            pre = rwkv_pre(rkv, hw, ha, None, w2, a2, None, vecs, None)
            v_first = rkv
        else:
            vecs = jnp.stack([rwkv_w0[i], rwkv_a0[i], rwkv_v0[i - 1], rwkv_k_k[i], rwkv_k_a[i],
                              r_k, zeros, zeros])
            v2 = bf(_pad_rows(rwkv_v2[i - 1], lora_pad))
            pre = rwkv_pre(rkv, hw, ha, hv[0], w2, a2, v2, vecs, v_first)
        *rec_in, bonus = pre
        y = rwkv_recurrence(*rec_in)
        y = rwkv_post(y, bonus, hg, bf(rwkv_g2[i]), rwkv_lnx_w[i:i + 1], rwkv_lnx_b[i:i + 1])
        xs = matmul(y, rwkv_w_o, w_idx=(i,), res=xs, name="rwkv_out")
        xs = mlp_block(xs, ffn_norm[i:i + 1], w_up_bf, w_down_bf, i)

    kv = vt = c = None
    for j in range(n_b):
        layer = n_a + j
        if j == 0:
            normed = rms_norm_multi(xs, jnp.stack([mix_norm[layer], kv_norm]), BF16)
            kv = matmul(normed, w_kvf, x_idx=1, n_out=2 * D, out_dtype=BF16, name="kv_proj")
            b_pad = jnp.pad(b_f, (0, V7X_LANES - n_fox_heads)).reshape(1, V7X_LANES)
            log_f = matmul(normed, bf(_pad_cols(w_kvf[:, 2 * D:], V7X_LANES)), x_idx=1,
                           bias=b_pad, act=_log_sigmoid, name="forget_proj")
            c = cumsum_rows(log_f)
            vt = kv[:, D:].reshape(S // fox_tk, fox_tk, n_fox_heads, FOX_HEAD).transpose(2, 0, 3, 1)
        else:
            normed = rms_norm_multi(xs, mix_norm[layer:layer + 1], BF16)
        q = matmul(normed, fox_w_q, w_idx=(j,), x_idx=0, scale=LOG2_E * FOX_HEAD ** -0.5,
                   out_dtype=BF16, name="q_proj")
        o = fox_attention(q, kv, vt, c, n_fox_heads, tk=fox_tk)
        xs = matmul(o, fox_w_o, w_idx=(j,), res=xs, name="fox_out")
        xs = mlp_block(xs, ffn_norm[layer:layer + 1], w_up_bf, w_down_bf, layer)

    out = rms_norm_multi(xs, final_norm.reshape(1, D), F32)
    return out.reshape(B, S, D)
```

```python
import functools
import math

import jax
import jax.numpy as jnp
from jax import lax
from jax.experimental import pallas as pl
from jax.experimental.pallas import tpu as pltpu

F32 = jnp.float32
BF16 = jnp.bfloat16

RWKV_HEAD = 64
FOX_HEAD = 128
RMS_EPS = 1e-6
LNX_EPS = 1e-5 * RWKV_HEAD
V7X_LANES = 128
REC_CHUNK = 64
V7X_VMEM_LIMIT = 56 * 1024 * 1024
MASK_VALUE = -1e30
LOG2_E = math.log2(math.e)
DECAY_SCALE = math.exp(-0.5)


def _cparams(*sem):
    return pltpu.CompilerParams(dimension_semantics=sem, vmem_limit_bytes=V7X_VMEM_LIMIT)


def _sigmoid(z):
    return 1.0 / (1.0 + jnp.exp(-z))


def _log_sigmoid(z):
    return jnp.minimum(z, 0.0) - jnp.log(1.0 + jnp.exp(-jnp.abs(z)))


def _rms(x, gain):
    return x * lax.rsqrt(jnp.mean(x * x, axis=-1, keepdims=True) + RMS_EPS) * gain


def _head_sum_matrix(n, head):
    r = lax.broadcasted_iota(jnp.int32, (n, n), 0) // head
    c = lax.broadcasted_iota(jnp.int32, (n, n), 1) // head
    return (r == c).astype(BF16)


def _group_sum(x, ones_bd):
    return jnp.dot(x.astype(BF16), ones_bd, preferred_element_type=F32)


def _tri_sum(tri, x):
    hi = x.astype(BF16)
    r1 = x - hi.astype(F32)
    mid = r1.astype(BF16)
    lo = (r1 - mid.astype(F32)).astype(BF16)
    return (jnp.dot(tri, hi, preferred_element_type=F32)
            + jnp.dot(tri, mid, preferred_element_type=F32)
            + jnp.dot(tri, lo, preferred_element_type=F32))


def _mix_kernel(x_ref, halo_ref, g_ref, mix_ref, w1_ref, a1_ref, g1_ref, *rest, has_vres):
    if has_vres:
        v1_ref, o_ref, hw_ref, ha_ref, hg_ref, hv_ref = rest
    else:
        o_ref, hw_ref, ha_ref, hg_ref = rest
    i = pl.program_id(0)
    gain = g_ref[...]
    h = _rms(x_ref[...], gain)
    hh = _rms(halo_ref[...], gain)
    last = jnp.where(i == 0, 0.0, hh[7:8, :])
    row = lax.broadcasted_iota(jnp.int32, h.shape, 0)
    prev = jnp.where(row == 0, last, pltpu.roll(h, 1, 0))
    d = prev - h
    mixed = lambda n: (h + d * mix_ref[n:n + 1, :]).astype(BF16)
    lora = lambda n, w_ref: jnp.dot(mixed(n), w_ref[...], preferred_element_type=F32)
    xv = mixed(2)
    o_ref[0] = mixed(0)
    o_ref[1] = mixed(1)
    o_ref[2] = xv
    hw_ref[...] = jnp.tanh(lora(3, w1_ref)).astype(BF16)
    ha_ref[...] = lora(4, a1_ref).astype(BF16)
    hg_ref[...] = _sigmoid(lora(5, g1_ref)).astype(BF16)
    if has_vres:
        hv_ref[...] = jnp.dot(xv, v1_ref[...], preferred_element_type=F32).astype(BF16)


def rwkv_mix(x, gain, x_mix, w1, a1, g1, v1, tm=256):
    S, D = x.shape
    tm = min(tm, S)
    hb = tm // 8
    has_vres = v1 is not None
    row = lambda i: (i, 0)
    full = lambda i: (0, 0)
    loras = [w1, a1, g1] + ([v1] if has_vres else [])
    return pl.pallas_call(
        functools.partial(_mix_kernel, has_vres=has_vres),
        grid=(S // tm,),
        in_specs=[pl.BlockSpec((tm, D), row),
                  pl.BlockSpec((8, D), lambda i: (jnp.maximum(i * hb - 1, 0), 0)),
                  pl.BlockSpec((1, D), full),
                  pl.BlockSpec(x_mix.shape, full)]
                 + [pl.BlockSpec(w.shape, full) for w in loras],
        out_specs=[pl.BlockSpec((3, tm, D), lambda i: (0, i, 0))]
                  + [pl.BlockSpec((tm, w.shape[1]), row) for w in loras],
        out_shape=[jax.ShapeDtypeStruct((3, S, D), BF16)]
                  + [jax.ShapeDtypeStruct((S, w.shape[1]), BF16) for w in loras],
        compiler_params=_cparams("parallel"),
        name="rwkv_mix",
    )(x, x, gain, x_mix, *loras)


def _mm_kernel(x_ref, w_ref, *rest, act, scale, has_bias, has_res, cast_w, row_axis):
    if cast_w:
        o_ref, wbf_ref = rest[-2], rest[-1]

        @pl.when(pl.program_id(row_axis) == 0)
        def _():
            wbf_ref[...] = w_ref[...].astype(BF16)
        w = wbf_ref[...]
    else:
        o_ref = rest[-1]
        w = w_ref[...]
    acc = jnp.dot(x_ref[...], w, preferred_element_type=F32)
    if scale != 1.0:
        acc = acc * scale
    if has_bias:
        acc = acc + rest[0][...]
    if act is not None:
        acc = act(acc)
    if has_res:
        acc = acc + rest[1 if has_bias else 0][...]
    o_ref[...] = acc.astype(o_ref.dtype)


def matmul(x, w, *, x_idx=None, w_idx=(), n_out=None, act=None, scale=1.0, bias=None, res=None,
           out_dtype=F32, tm=1024, tn=1024, name="matmul"):
    M, K = x.shape[-2:]
    N = w.shape[-1] if n_out is None else n_out
    batched = w.ndim - len(w_idx) == 3
    tm, tn = min(tm, M), min(tn, N)
    cast_w = w.dtype != BF16
    lead = (None,) * len(w_idx)
    if batched:
        nb = w.shape[len(w_idx)]
        grid = (nb, N // tn, M // tm)
        x_spec = pl.BlockSpec((None, tm, K), lambda b, j, i: (b, i, 0))
        w_spec = pl.BlockSpec(lead + (None, K, tn), lambda b, j, i: w_idx + (b, 0, j))
        o_spec = pl.BlockSpec((None, tm, tn), lambda b, j, i: (b, i, j))
        out_shape = jax.ShapeDtypeStruct((nb, M, N), out_dtype)
        sem = ("parallel", "parallel", "arbitrary")
    else:
        grid = (N // tn, M // tm)
        if x_idx is None:
            x_spec = pl.BlockSpec((tm, K), lambda j, i: (i, 0))
        else:
            x_spec = pl.BlockSpec((None, tm, K), lambda j, i: (x_idx, i, 0))
        w_spec = pl.BlockSpec(lead + (K, tn), lambda j, i: w_idx + (0, j))
        o_spec = pl.BlockSpec((tm, tn), lambda j, i: (i, j))
        out_shape = jax.ShapeDtypeStruct((M, N), out_dtype)
        sem = ("parallel", "arbitrary")
    kern = functools.partial(_mm_kernel, act=act, scale=scale, has_bias=bias is not None,
                             has_res=res is not None, cast_w=cast_w, row_axis=len(grid) - 1)
    in_specs = [x_spec, w_spec]
    args = [x, w]
    if bias is not None:
        assert not batched
        in_specs.append(pl.BlockSpec((1, tn), lambda j, i: (0, j)))
        args.append(bias)
    if res is not None:
        assert not batched
        in_specs.append(o_spec)
        args.append(res)
    return pl.pallas_call(
        kern, grid=grid, in_specs=in_specs, out_specs=o_spec, out_shape=out_shape,
        scratch_shapes=[pltpu.VMEM((K, tn), BF16)] if cast_w else [],
        compiler_params=_cparams(*sem), name=name,
    )(*args)


def _pre_kernel(*refs, has_vres):
    if has_vres:
        (r_ref, k_ref, v_ref, hw_ref, ha_ref, hv_ref, w2_ref, a2_ref, v2_ref,
         vec_ref, vf_ref, *outs) = refs
    else:
        (r_ref, k_ref, v_ref, hw_ref, ha_ref, w2_ref, a2_ref, vec_ref, *outs) = refs
    ao_ref, ro_ref, bo_ref, ko_ref, bdo_ref, kdo_ref, vo_ref, pend_ref, bonus_ref = outs
    tm, D = k_ref.shape
    T, W = REC_CHUNK, V7X_LANES
    ri = lax.broadcasted_iota(jnp.int32, (tm, tm), 0)
    ci = lax.broadcasted_iota(jnp.int32, (tm, tm), 1)
    tri = jnp.logical_and(ci <= ri, ci // T == ri // T).astype(BF16)
    row = lax.broadcasted_iota(jnp.int32, (tm, D), 0)
    ones_bd = _head_sum_matrix(W, RWKV_HEAD)
    w0, a0, v0, k_k, k_a, r_k = (vec_ref[n:n + 1, :] for n in range(6))

    def head_sum(x):
        return jnp.concatenate([_group_sum(x[:, s * W:(s + 1) * W], ones_bd)
                                for s in range(D // W)], axis=1)

    r, k, v = r_ref[...], k_ref[...], v_ref[...]
    z_w = w0 + jnp.dot(hw_ref[...], w2_ref[...], preferred_element_type=F32)
    lw = -DECAY_SCALE * _sigmoid(z_w)
    a_sig = _sigmoid(a0 + jnp.dot(ha_ref[...], a2_ref[...], preferred_element_type=F32))
    if has_vres:
        mix = _sigmoid(v0 + jnp.dot(hv_ref[...], v2_ref[...], preferred_element_type=F32))
        v = v + (vf_ref[...] - v) * mix
    vo_ref[...] = v.astype(BF16)
    kmod = k * (1.0 + (a_sig - 1.0) * k_a)
    bonus_ref[...] = head_sum(r * kmod * r_k) * v
    kk = k * k_k
    kk = kk * lax.rsqrt(jnp.maximum(head_sum(kk * kk), 1e-24))
    b = kk * a_sig

    L = _tri_sum(tri, lw)
    l_end = L[T - 1:T, :]
    for c in range(1, tm // T):
        l_end = jnp.where(row < c * T, l_end, L[(c + 1) * T - 1:(c + 1) * T, :])
    for c in range(tm // T):
        pend_ref[c] = jnp.broadcast_to(jnp.exp(L[(c + 1) * T - 1:(c + 1) * T, :]), (8, D))
    e_pos = jnp.exp(L)
    e_neg = 1.0 / e_pos
    to_end = jnp.exp(l_end - L)
    ao_ref[...] = (-kk * jnp.exp(L - lw)).astype(BF16)
    ro_ref[...] = (r * e_pos).astype(BF16)
    bo_ref[...] = (b * e_neg).astype(BF16)
    ko_ref[...] = (kmod * e_neg).astype(BF16)
    bdo_ref[...] = (b * to_end).astype(BF16)
    kdo_ref[...] = (kmod * to_end).astype(BF16)


def rwkv_pre(rkv, hw, ha, hv, w2, a2, v2, vecs, v_first, tm=128):
    _, S, D = rkv.shape
    tm = min(tm, S)
    assert tm % REC_CHUNK == 0
    has_vres = v_first is not None
    row = lambda i: (i, 0)
    full = lambda i: (0, 0)
    act = pl.BlockSpec((tm, D), row)
    rkv_spec = lambda n: pl.BlockSpec((None, tm, D), lambda i: (n, i, 0))
    lora = lambda a: pl.BlockSpec((tm, a.shape[1]), row)
    wspec = lambda a: pl.BlockSpec(a.shape, full)
    if has_vres:
        in_specs = [rkv_spec(0), rkv_spec(1), rkv_spec(2), lora(hw), lora(ha), lora(hv),
                    wspec(w2), wspec(a2), wspec(v2), wspec(vecs), rkv_spec(2)]
        args = [rkv, rkv, rkv, hw, ha, hv, w2, a2, v2, vecs, v_first]
    else:
        in_specs = [rkv_spec(0), rkv_spec(1), rkv_spec(2), lora(hw), lora(ha),
                    wspec(w2), wspec(a2), wspec(vecs)]
        args = [rkv, rkv, rkv, hw, ha, w2, a2, vecs]
    n_chunks = S // REC_CHUNK
    pend_spec = pl.BlockSpec((tm // REC_CHUNK, 8, D), lambda i: (i, 0, 0))
    bf_out = jax.ShapeDtypeStruct((S, D), BF16)
    f32_out = jax.ShapeDtypeStruct((S, D), F32)
    return pl.pallas_call(
        functools.partial(_pre_kernel, has_vres=has_vres),
        grid=(S // tm,),
        in_specs=in_specs,
        out_specs=[act] * 7 + [pend_spec, act],
        out_shape=[bf_out] * 7 + [jax.ShapeDtypeStruct((n_chunks, 8, D), F32), f32_out],
        compiler_params=_cparams("parallel"),
        name="rwkv_pre",
    )(*args)


def _rec_kernel(a_ref, r_ref, b_ref, k_ref, bd_ref, kd_ref, v_ref, pend_ref, y_ref, st_ref):
    T, W = REC_CHUNK, V7X_LANES
    D = a_ref.shape[-1]
    slabs = range(D // W)

    @pl.when(pl.program_id(0) == 0)
    def _():
        st_ref[...] = jnp.zeros_like(st_ref)

    lane = lax.broadcasted_iota(jnp.int32, (T, W), 1)
    head0 = lane < RWKV_HEAD
    t_row = lax.broadcasted_iota(jnp.int32, (T, W), 0)
    strict = (lane & (T - 1)) < t_row
    incl = (lane & (T - 1)) <= t_row
    same_head = (lax.broadcasted_iota(jnp.int32, (W, W), 0) // RWKV_HEAD
                 == lax.broadcasted_iota(jnp.int32, (W, W), 1) // RWKV_HEAD)

    def stacked(x):
        x = x.astype(F32)
        return jnp.concatenate([jnp.where(head0, x, 0.0), jnp.where(head0, 0.0, x)],
                               axis=0).astype(BF16)

    def tile(ref, p):
        return ref[:, p * W:(p + 1) * W]

    def mm(x, y):
        return jnp.dot(x.astype(BF16), y.astype(BF16), preferred_element_type=F32)

    def mm_nt(x, y):
        return lax.dot_general(x.astype(BF16), y.astype(BF16), (((1,), (1,)), ((), ())),
                               preferred_element_type=F32)

    def mm_tn(x, y):
        return lax.dot_general(x.astype(BF16), y.astype(BF16), (((0,), (0,)), ((), ())),
                               preferred_element_type=F32)

    ar = [jnp.concatenate([tile(a_ref, p), tile(r_ref, p)], axis=0) for p in slabs]
    gram = [mm_nt(ar[p], jnp.concatenate([stacked(tile(b_ref, p)), stacked(tile(k_ref, p))],
                                         axis=0)) for p in slabs]
    n_w = [jnp.where(strict, g[:T, :W], 0.0) for g in gram]
    a_rb = [jnp.where(incl, g[T:, :W], 0.0).astype(BF16) for g in gram]
    av = [mm(jnp.concatenate([jnp.where(strict, g[:T, W:], 0.0),
                              jnp.where(incl, g[T:, W:], 0.0)], axis=0),
             stacked(tile(v_ref, p))) for p, g in zip(slabs, gram)]

    s0 = [st_ref[p] for p in slabs]
    wr = [mm_nt(ar[p], s0[p]) for p in slabs]

    u_w = [wr[p][:T] + av[p][:T] for p in slabs]
    for _ in range(5):
        z = [mm(n_w[p], jnp.concatenate([stacked(n_w[p]), stacked(u_w[p])], axis=1))
             for p in slabs]
        n_w = [zp[:, :W] for zp in z]
        u_w = [u_w[p] + z[p][:, W:] for p in slabs]
    u_w = [u_w[p] + mm(n_w[p], stacked(u_w[p])) for p in slabs]
    for p in slabs:
        upd = mm_tn(jnp.concatenate([u_w[p].astype(BF16), tile(v_ref, p)], axis=0),
                    jnp.concatenate([tile(bd_ref, p), tile(kd_ref, p)], axis=0))
        st_ref[p] = s0[p] * tile(pend_ref, p)[0:1, :] + jnp.where(same_head, upd, 0.0)
    for p in slabs:
        y_ref[:, p * W:(p + 1) * W] = wr[p][T:] + mm(a_rb[p], stacked(u_w[p])) + av[p][T:]


def rwkv_recurrence(a, r, b, k, bd, kd, v, pend):
    S, D = a.shape
    T = REC_CHUNK
    act = pl.BlockSpec((T, D), lambda c: (c, 0))
    return pl.pallas_call(
        _rec_kernel,
        grid=(S // T,),
        in_specs=[act] * 7 + [pl.BlockSpec((None, 8, D), lambda c: (c, 0, 0))],
        out_specs=act,
        out_shape=jax.ShapeDtypeStruct((S, D), F32),
        scratch_shapes=[pltpu.VMEM((D // V7X_LANES, V7X_LANES, V7X_LANES), F32)],
        compiler_params=_cparams("arbitrary"),
        name="rwkv_recurrence",
    )(a, r, b, k, bd, kd, v, pend)


def _post_kernel(y_ref, bonus_ref, hg_ref, g2_ref, lnw_ref, lnb_ref, o_ref):
    W = V7X_LANES
    ones_bd = _head_sum_matrix(W, RWKV_HEAD)
    inv_n = 1.0 / RWKV_HEAD

    def head_sum(x):
        return jnp.concatenate([_group_sum(x[:, s * W:(s + 1) * W], ones_bd)
                                for s in range(x.shape[-1] // W)], axis=1)

    y = y_ref[...]
    d = y - head_sum(y) * inv_n
    var = head_sum(d * d) * inv_n
    yn = d * lax.rsqrt(var + LNX_EPS) * lnw_ref[...] + lnb_ref[...]
    gate = jnp.dot(hg_ref[...], g2_ref[...], preferred_element_type=F32)
    o_ref[...] = ((yn + bonus_ref[...]) * gate).astype(o_ref.dtype)


def rwkv_post(y, bonus, hg, g2, lnw, lnb, tm=256):
    S, D = y.shape
    tm = min(tm, S)
    act = pl.BlockSpec((tm, D), lambda i: (i, 0))
    vec = pl.BlockSpec((1, D), lambda i: (0, 0))
    return pl.pallas_call(
        _post_kernel,
        grid=(S // tm,),
        in_specs=[act, act, pl.BlockSpec((tm, hg.shape[1]), lambda i: (i, 0)),
                  pl.BlockSpec(g2.shape, lambda i: (0, 0)), vec, vec],
        out_specs=act,
        out_shape=jax.ShapeDtypeStruct((S, D), BF16),
        compiler_params=_cparams("parallel"),
        name="rwkv_post",
    )(y, bonus, hg, g2, lnw, lnb)


def _mlp_kernel(x_ref, g_ref, wu_ref, wd_ref, *rest, emit_x, has_post):
    if not has_post:
        acc_ref, xn_ref = rest
    elif emit_x:
        pg_ref, acc_ref, post_ref, xn_ref = rest
    else:
        pg_ref, post_ref, xn_ref, acc_ref = rest
    f = pl.program_id(1)

    @pl.when(f == 0)
    def _():
        x = x_ref[...]
        xn_ref[...] = _rms(x, g_ref[...]).astype(BF16)
        acc_ref[...] = x

    h = jnp.dot(xn_ref[...], wu_ref[...], preferred_element_type=F32)
    h = jnp.square(jnp.maximum(h, 0.0)).astype(BF16)
    acc_ref[...] += jnp.dot(h, wd_ref[...], preferred_element_type=F32)

    if has_post:
        @pl.when(f == pl.num_programs(1) - 1)
        def _():
            y = acc_ref[...]
            yn = y * lax.rsqrt(jnp.mean(y * y, axis=-1, keepdims=True) + RMS_EPS)
            for n in range(post_ref.shape[0]):
                post_ref[n] = (yn * pg_ref[n:n + 1, :]).astype(post_ref.dtype)


def mlp_block(x, gain, w_up, w_down, layer, post_gains=None, post_dtype=BF16, emit_x=True,
              tm=512, tf=1024):
    S, D = x.shape
    F = w_up.shape[-1]
    has_post = post_gains is not None
    assert has_post or emit_x
    tm, tf = min(tm, S), min(tf, F)
    row = pl.BlockSpec((tm, D), lambda i, f: (i, 0))
    in_specs = [row,
                pl.BlockSpec((1, D), lambda i, f: (0, 0)),
                pl.BlockSpec((None, D, tf), lambda i, f: (layer, 0, f)),
                pl.BlockSpec((None, tf, D), lambda i, f: (layer, f, 0))]
    args = [x, gain, w_up, w_down]
    out_specs = [row] if emit_x else []
    out_shape = [jax.ShapeDtypeStruct((S, D), F32)] if emit_x else []
    if has_post:
        G = post_gains.shape[0]
        in_specs.append(pl.BlockSpec((G, D), lambda i, f: (0, 0)))
        args.append(post_gains)
        out_specs.append(pl.BlockSpec((G, tm, D), lambda i, f: (0, i, 0)))
        out_shape.append(jax.ShapeDtypeStruct((G, S, D), post_dtype))
    outs = pl.pallas_call(
        functools.partial(_mlp_kernel, emit_x=emit_x, has_post=has_post),
        grid=(S // tm, F // tf),
        in_specs=in_specs,
        out_specs=out_specs,
        out_shape=out_shape,
        scratch_shapes=[pltpu.VMEM((tm, D), BF16)] + ([] if emit_x else [pltpu.VMEM((tm, D), F32)]),
        compiler_params=_cparams("parallel", "arbitrary"),
        name="mlp",
    )(*args)
    return outs if len(outs) > 1 else outs[0]


def _cumsum_kernel(x_ref, o_ref, carry_ref):
    @pl.when(pl.program_id(0) == 0)
    def _():
        carry_ref[...] = jnp.zeros_like(carry_ref)

    n = x_ref.shape[0]
    tri = (lax.broadcasted_iota(jnp.int32, (n, n), 1)
           <= lax.broadcasted_iota(jnp.int32, (n, n), 0)).astype(BF16)
    c = _tri_sum(tri, x_ref[...]) + carry_ref[0:1, :]
    o_ref[...] = c
    carry_ref[...] = jnp.broadcast_to(c[n - 1:n, :], carry_ref.shape)


def cumsum_rows(x, tc=256):
    S, W = x.shape
    tc = min(tc, S)
    return pl.pallas_call(
        _cumsum_kernel,
        grid=(S // tc,),
        in_specs=[pl.BlockSpec((tc, W), lambda i: (i, 0))],
        out_specs=pl.BlockSpec((tc, W), lambda i: (i, 0)),
        out_shape=jax.ShapeDtypeStruct((S, W), F32),
        scratch_shapes=[pltpu.VMEM((8, W), F32)],
        compiler_params=_cparams("arbitrary"),
        name="forget_cumsum",
    )(x)


FOX_ONES_ROWS = 16


def _fox_kernel(q_ref, k_ref, vt_ref, c_ref, o_ref, ck_ref, m_ref, acc_ref, s_ref, *, tq, tk):
    h = pl.program_id(0)
    qi = pl.program_id(1)
    nsub = tq // tk
    hd = k_ref.shape[1]
    S = k_ref.shape[0]

    @pl.when(qi == 0)
    def _():
        lane = lax.broadcasted_iota(jnp.int32, (tk, V7X_LANES), 1)
        for j in range(S // tk):
            col = jnp.sum(jnp.where(lane == h, c_ref[j * tk:(j + 1) * tk, :], 0.0),
                          axis=1, keepdims=True)
            ck_ref[j * tk:(j + 1) * tk, :] = jnp.broadcast_to(col * LOG2_E, (tk, V7X_LANES))

    m_ref[...] = jnp.full_like(m_ref, MASK_VALUE)
    acc_ref[...] = jnp.zeros_like(acc_ref)
    q = q_ref[...]
    ones_rows = jnp.ones((FOX_ONES_ROWS, tk), BF16)

    def scores(kj, slot, lo=0):
        start = pl.multiple_of(kj * tk, tk)
        s_ref[slot, :, lo:] = lax.dot_general(
            k_ref[pl.ds(start, tk), :], q[lo:], (((1,), (1,)), ((), ())),
            preferred_element_type=F32)

    def update(kj, slot, lo=0, hi=tq, diagonal=False):
        start = pl.multiple_of(kj * tk, tk)
        ck = ck_ref[pl.ds(start, tk), :]
        s = s_ref[slot, :, lo:hi] - jnp.concatenate([ck] * ((hi - lo) // V7X_LANES), axis=1)
        if diagonal:
            key = lax.broadcasted_iota(jnp.int32, (tk, tk), 0)
            qry = lax.broadcasted_iota(jnp.int32, (tk, tk), 1)
            s = jnp.where(qry >= key, s, MASK_VALUE)
        m_prev = m_ref[0:1, lo:hi]
        m_new = jnp.maximum(m_prev, jnp.max(s, axis=0, keepdims=True))
        alpha = jnp.exp2(m_prev - m_new)
        p = jnp.exp2(s - m_new).astype(BF16)
        vt = jnp.concatenate([vt_ref[kj], ones_rows], axis=0)
        acc_ref[:, lo:hi] = (alpha * acc_ref[:, lo:hi]
                             + jnp.dot(vt, p, preferred_element_type=F32))
        m_ref[0:1, lo:hi] = m_new

    assert nsub == 2
    scores(0, 0)

    def full_blocks(i, carry):
        scores(2 * i + 1, 1)
        update(2 * i, 0)
        scores(2 * i + 2, 0)
        update(2 * i + 1, 1)
        return carry

    lax.fori_loop(0, qi, full_blocks, 0)
    scores(2 * qi + 1, 1, lo=tk)
    update(2 * qi, 0, 0, tk, diagonal=True)
    update(2 * qi, 0, tk, tq)
    update(2 * qi + 1, 1, tk, tq, diagonal=True)
    out_t = acc_ref[0:hd, :] / acc_ref[hd:hd + 1, :]
    o_ref[...] = out_t.T.astype(o_ref.dtype)


def fox_attention(q, k, vt, c, n_heads, tq=1024, tk=512):
    S = q.shape[0]
    hd = FOX_HEAD
    tq = min(tq, S)
    assert tq % tk == 0 and vt.shape[-1] == tk
    return pl.pallas_call(
        functools.partial(_fox_kernel, tq=tq, tk=tk),
        grid=(n_heads, S // tq),
        in_specs=[pl.BlockSpec((tq, hd), lambda h, i: (i, h)),
                  pl.BlockSpec((S, hd), lambda h, i: (0, h)),
                  pl.BlockSpec((None, S // tk, hd, tk), lambda h, i: (h, 0, 0, 0)),
                  pl.BlockSpec((S, V7X_LANES), lambda h, i: (0, 0))],
        out_specs=pl.BlockSpec((tq, hd), lambda h, i: (i, h)),
        out_shape=jax.ShapeDtypeStruct((S, n_heads * hd), BF16),
        scratch_shapes=[pltpu.VMEM((S, V7X_LANES), F32),
                        pltpu.VMEM((8, tq), F32),
                        pltpu.VMEM((hd + FOX_ONES_ROWS, tq), F32),
                        pltpu.VMEM((2, tk, tq), F32)],
        compiler_params=_cparams("parallel", "arbitrary"),
        name="fox_attention",
    )(q, k, vt, c)


def _pad_cols(w, n):
    return jnp.pad(w, ((0, 0), (0, n - w.shape[1])))


def _pad_rows(w, n):
    return jnp.pad(w, ((0, n - w.shape[0]), (0, 0)))


def kernel(x, mix_norm, ffn_norm, final_norm, rwkv_x_mix, rwkv_w_rkv, rwkv_w0, rwkv_w1, rwkv_w2, rwkv_a0, rwkv_a1, rwkv_a2, rwkv_v0, rwkv_v1, rwkv_v2, rwkv_g1, rwkv_g2, rwkv_k_k, rwkv_k_a, rwkv_r_k, rwkv_lnx_w, rwkv_lnx_b, rwkv_w_o, kv_norm, w_kvf, b_f, fox_w_q, fox_w_o, mlp_w_up, mlp_w_down):
    B, S, D = x.shape
    assert B == 1
    n_a = rwkv_x_mix.shape[0]
    n_b = fox_w_q.shape[0]
    assert n_a >= 1 and n_b >= 1
    n_fox_heads = D // FOX_HEAD
    lora_pad = V7X_LANES
    fox_tk = min(512, S)
    xs = x.reshape(S, D)
    bf = lambda w: w.astype(BF16)
    w_up_bf, w_down_bf = bf(mlp_w_up), bf(mlp_w_down)

    v_first = None
    for i in range(n_a):
        w1 = bf(_pad_cols(rwkv_w1[i], lora_pad))
        a1 = bf(_pad_cols(rwkv_a1[i], lora_pad))
        v1 = bf(_pad_cols(rwkv_v1[i - 1], lora_pad)) if i > 0 else None
        mixed, hw, ha, hg, *hv = rwkv_mix(xs, mix_norm[i:i + 1], rwkv_x_mix[i], w1, a1,
                                          bf(rwkv_g1[i]), v1)
        rkv = matmul(mixed, rwkv_w_rkv, w_idx=(i,), name="rkv_proj")
        w2 = bf(_pad_rows(rwkv_w2[i], lora_pad))
        a2 = bf(_pad_rows(rwkv_a2[i], lora_pad))
        zeros = jnp.zeros((D,), F32)
        r_k = rwkv_r_k[i].reshape(D)
        if i == 0:
            vecs = jnp.stack([rwkv_w0[i], rwkv_a0[i], zeros, rwkv_k_k[i], rwkv_k_a[i], r_k,
                              zeros, zeros])
            pre = rwkv_pre(rkv, hw, ha, None, w2, a2, None, vecs, None)
            v_first = rkv
        else:
            vecs = jnp.stack([rwkv_w0[i], rwkv_a0[i], rwkv_v0[i - 1], rwkv_k_k[i], rwkv_k_a[i],
                              r_k, zeros, zeros])
            v2 = bf(_pad_rows(rwkv_v2[i - 1], lora_pad))
            pre = rwkv_pre(rkv, hw, ha, hv[0], w2, a2, v2, vecs, v_first)
        *rec_in, bonus = pre
        y = rwkv_recurrence(*rec_in)
        y = rwkv_post(y, bonus, hg, bf(rwkv_g2[i]), rwkv_lnx_w[i:i + 1], rwkv_lnx_b[i:i + 1])
        xs = matmul(y, rwkv_w_o, w_idx=(i,), res=xs, name="rwkv_out")
        if i + 1 < n_a:
            xs = mlp_block(xs, ffn_norm[i:i + 1], w_up_bf, w_down_bf, i)
        else:
            xs, normed = mlp_block(xs, ffn_norm[i:i + 1], w_up_bf, w_down_bf, i,
                                   post_gains=jnp.stack([mix_norm[n_a], kv_norm]))

    kv = vt = c = out = None
    for j in range(n_b):
        layer = n_a + j
        if j == 0:
            kv = matmul(normed, w_kvf, x_idx=1, n_out=2 * D, out_dtype=BF16, name="kv_proj")
            b_pad = jnp.pad(b_f, (0, V7X_LANES - n_fox_heads)).reshape(1, V7X_LANES)
            log_f = matmul(normed, bf(_pad_cols(w_kvf[:, 2 * D:], V7X_LANES)), x_idx=1,
                           bias=b_pad, act=_log_sigmoid, name="forget_proj")
            c = cumsum_rows(log_f)
            vt = kv[:, D:].reshape(S // fox_tk, fox_tk, n_fox_heads, FOX_HEAD).transpose(2, 0, 3, 1)
        q = matmul(normed, fox_w_q, w_idx=(j,), x_idx=0, scale=LOG2_E * FOX_HEAD ** -0.5,
                   out_dtype=BF16, name="q_proj")
        o = fox_attention(q, kv, vt, c, n_fox_heads, tk=fox_tk)
        xs = matmul(o, fox_w_o, w_idx=(j,), res=xs, name="fox_out")
        if j + 1 < n_b:
            xs, normed = mlp_block(xs, ffn_norm[layer:layer + 1], w_up_bf, w_down_bf, layer,
                                   post_gains=mix_norm[layer + 1:layer + 2])
        else:
            out = mlp_block(xs, ffn_norm[layer:layer + 1], w_up_bf, w_down_bf, layer,
                            post_gains=final_norm.reshape(1, D), post_dtype=F32, emit_x=False)
    return out.reshape(B, S, D)
```

```python
import functools
import math

import jax
import jax.numpy as jnp
from jax import lax
from jax.experimental import pallas as pl
from jax.experimental.pallas import tpu as pltpu

F32 = jnp.float32
BF16 = jnp.bfloat16

RWKV_HEAD = 64
FOX_HEAD = 128
RMS_EPS = 1e-6
LNX_EPS = 1e-5 * RWKV_HEAD
V7X_LANES = 128
REC_CHUNK = 64
V7X_VMEM_LIMIT = 56 * 1024 * 1024
MASK_VALUE = -1e30
LOG2_E = math.log2(math.e)
DECAY_SCALE = math.exp(-0.5)


def _cparams(*sem):
    return pltpu.CompilerParams(dimension_semantics=sem, vmem_limit_bytes=V7X_VMEM_LIMIT)


def _sigmoid(z):
    return 1.0 / (1.0 + jnp.exp(-z))


def _log_sigmoid(z):
    return jnp.minimum(z, 0.0) - jnp.log(1.0 + jnp.exp(-jnp.abs(z)))


def _rms(x, gain):
    return x * lax.rsqrt(jnp.mean(x * x, axis=-1, keepdims=True) + RMS_EPS) * gain


def _head_sum_matrix(n, head):
    r = lax.broadcasted_iota(jnp.int32, (n, n), 0) // head
    c = lax.broadcasted_iota(jnp.int32, (n, n), 1) // head
    return (r == c).astype(BF16)


def _group_sum(x, ones_bd):
    return jnp.dot(x.astype(BF16), ones_bd, preferred_element_type=F32)


def _tri_sum(tri, x):
    hi = x.astype(BF16)
    r1 = x - hi.astype(F32)
    mid = r1.astype(BF16)
    lo = (r1 - mid.astype(F32)).astype(BF16)
    return (jnp.dot(tri, hi, preferred_element_type=F32)
            + jnp.dot(tri, mid, preferred_element_type=F32)
            + jnp.dot(tri, lo, preferred_element_type=F32))


def _mix_kernel(x_ref, halo_ref, g_ref, mix_ref, w1_ref, a1_ref, g1_ref, *rest, has_vres):
    if has_vres:
        v1_ref, o_ref, hw_ref, ha_ref, hg_ref, hv_ref = rest
    else:
        o_ref, hw_ref, ha_ref, hg_ref = rest
    i = pl.program_id(0)
    gain = g_ref[...]
    h = _rms(x_ref[...], gain)
    hh = _rms(halo_ref[...], gain)
    last = jnp.where(i == 0, 0.0, hh[7:8, :])
    row = lax.broadcasted_iota(jnp.int32, h.shape, 0)
    prev = jnp.where(row == 0, last, pltpu.roll(h, 1, 0))
    d = prev - h
    mixed = lambda n: (h + d * mix_ref[n:n + 1, :]).astype(BF16)
    lora = lambda n, w_ref: jnp.dot(mixed(n), w_ref[...], preferred_element_type=F32)
    xv = mixed(2)
    o_ref[0] = mixed(0)
    o_ref[1] = mixed(1)
    o_ref[2] = xv
    hw_ref[...] = jnp.tanh(lora(3, w1_ref)).astype(BF16)
    ha_ref[...] = lora(4, a1_ref).astype(BF16)
    hg_ref[...] = _sigmoid(lora(5, g1_ref)).astype(BF16)
    if has_vres:
        hv_ref[...] = jnp.dot(xv, v1_ref[...], preferred_element_type=F32).astype(BF16)


def rwkv_mix(x, gain, x_mix, w1, a1, g1, v1, tm=256):
    S, D = x.shape
    tm = min(tm, S)
    hb = tm // 8
    has_vres = v1 is not None
    row = lambda i: (i, 0)
    full = lambda i: (0, 0)
    loras = [w1, a1, g1] + ([v1] if has_vres else [])
    return pl.pallas_call(
        functools.partial(_mix_kernel, has_vres=has_vres),
        grid=(S // tm,),
        in_specs=[pl.BlockSpec((tm, D), row),
                  pl.BlockSpec((8, D), lambda i: (jnp.maximum(i * hb - 1, 0), 0)),
                  pl.BlockSpec((1, D), full),
                  pl.BlockSpec(x_mix.shape, full)]
                 + [pl.BlockSpec(w.shape, full) for w in loras],
        out_specs=[pl.BlockSpec((3, tm, D), lambda i: (0, i, 0))]
                  + [pl.BlockSpec((tm, w.shape[1]), row) for w in loras],
        out_shape=[jax.ShapeDtypeStruct((3, S, D), BF16)]
                  + [jax.ShapeDtypeStruct((S, w.shape[1]), BF16) for w in loras],
        compiler_params=_cparams("parallel"),
        name="rwkv_mix",
    )(x, x, gain, x_mix, *loras)


def _mm_kernel(x_ref, w_ref, *rest, act, scale, has_bias, has_res, cast_w, row_axis):
    if cast_w:
        o_ref, wbf_ref = rest[-2], rest[-1]

        @pl.when(pl.program_id(row_axis) == 0)
        def _():
            wbf_ref[...] = w_ref[...].astype(BF16)
        w = wbf_ref[...]
    else:
        o_ref = rest[-1]
        w = w_ref[...]
    acc = jnp.dot(x_ref[...], w, preferred_element_type=F32)
    if scale != 1.0:
        acc = acc * scale
    if has_bias:
        acc = acc + rest[0][...]
    if act is not None:
        acc = act(acc)
    if has_res:
        acc = acc + rest[1 if has_bias else 0][...]
    o_ref[...] = acc.astype(o_ref.dtype)


def matmul(x, w, *, x_idx=None, w_idx=(), n_out=None, act=None, scale=1.0, bias=None, res=None,
           out_dtype=F32, tm=1024, tn=1024, name="matmul"):
    M, K = x.shape[-2:]
    N = w.shape[-1] if n_out is None else n_out
    batched = w.ndim - len(w_idx) == 3
    tm, tn = min(tm, M), min(tn, N)
    cast_w = w.dtype != BF16
    lead = (None,) * len(w_idx)
    if batched:
        nb = w.shape[len(w_idx)]
        grid = (nb, N // tn, M // tm)
        x_spec = pl.BlockSpec((None, tm, K), lambda b, j, i: (b, i, 0))
        w_spec = pl.BlockSpec(lead + (None, K, tn), lambda b, j, i: w_idx + (b, 0, j))
        o_spec = pl.BlockSpec((None, tm, tn), lambda b, j, i: (b, i, j))
        out_shape = jax.ShapeDtypeStruct((nb, M, N), out_dtype)
        sem = ("parallel", "parallel", "arbitrary")
    else:
        grid = (N // tn, M // tm)
        if x_idx is None:
            x_spec = pl.BlockSpec((tm, K), lambda j, i: (i, 0))
        else:
            x_spec = pl.BlockSpec((None, tm, K), lambda j, i: (x_idx, i, 0))
        w_spec = pl.BlockSpec(lead + (K, tn), lambda j, i: w_idx + (0, j))
        o_spec = pl.BlockSpec((tm, tn), lambda j, i: (i, j))
        out_shape = jax.ShapeDtypeStruct((M, N), out_dtype)
        sem = ("parallel", "arbitrary")
    kern = functools.partial(_mm_kernel, act=act, scale=scale, has_bias=bias is not None,
                             has_res=res is not None, cast_w=cast_w, row_axis=len(grid) - 1)
    in_specs = [x_spec, w_spec]
    args = [x, w]
    if bias is not None:
        assert not batched
        in_specs.append(pl.BlockSpec((1, tn), lambda j, i: (0, j)))
        args.append(bias)
    if res is not None:
        assert not batched
        in_specs.append(o_spec)
        args.append(res)
    return pl.pallas_call(
        kern, grid=grid, in_specs=in_specs, out_specs=o_spec, out_shape=out_shape,
        scratch_shapes=[pltpu.VMEM((K, tn), BF16)] if cast_w else [],
        compiler_params=_cparams(*sem), name=name,
    )(*args)


def _proj_t_kernel(x_ref, w_ref, o_ref, wbf_ref):
    @pl.when(pl.program_id(1) == 0)
    def _():
        wbf_ref[...] = w_ref[...].astype(BF16)

    acc = jnp.dot(x_ref[...], wbf_ref[...], preferred_element_type=F32)
    n_heads, n_blocks, hd, tk = o_ref.shape
    for h in range(n_heads):
        for b in range(n_blocks):
            tile = acc[b * tk:(b + 1) * tk, h * hd:(h + 1) * hd]
            o_ref[h, b] = tile.T.astype(o_ref.dtype)


def matmul_head_transposed(x, x_idx, w, col0, n_heads, hd, tk, tm=1024, tn=1024):
    _, M, K = x.shape
    tm = min(tm, M)
    assert col0 % tn == 0 and (n_heads * hd) % tn == 0 and tn % hd == 0 and tm % tk == 0
    return pl.pallas_call(
        _proj_t_kernel,
        grid=(n_heads * hd // tn, M // tm),
        in_specs=[pl.BlockSpec((None, tm, K), lambda j, i: (x_idx, i, 0)),
                  pl.BlockSpec((K, tn), lambda j, i: (0, col0 // tn + j))],
        out_specs=pl.BlockSpec((tn // hd, tm // tk, hd, tk), lambda j, i: (j, i, 0, 0)),
        out_shape=jax.ShapeDtypeStruct((n_heads, M // tk, hd, tk), BF16),
        scratch_shapes=[pltpu.VMEM((K, tn), BF16)],
        compiler_params=_cparams("parallel", "arbitrary"),
        name="proj_head_transposed",
    )(x, w)


def _pre_kernel(*refs, has_vres):
    if has_vres:
        (r_ref, k_ref, v_ref, hw_ref, ha_ref, hv_ref, w2_ref, a2_ref, v2_ref,
         vec_ref, vf_ref, *outs) = refs
    else:
        (r_ref, k_ref, v_ref, hw_ref, ha_ref, w2_ref, a2_ref, vec_ref, *outs) = refs
    ao_ref, ro_ref, bo_ref, ko_ref, bdo_ref, kdo_ref, vo_ref, pend_ref, bonus_ref = outs
    tm, D = k_ref.shape
    T, W = REC_CHUNK, V7X_LANES
    ri = lax.broadcasted_iota(jnp.int32, (tm, tm), 0)
    ci = lax.broadcasted_iota(jnp.int32, (tm, tm), 1)
    tri = jnp.logical_and(ci <= ri, ci // T == ri // T).astype(BF16)
    row = lax.broadcasted_iota(jnp.int32, (tm, D), 0)
    ones_bd = _head_sum_matrix(W, RWKV_HEAD)
    w0, a0, v0, k_k, k_a, r_k = (vec_ref[n:n + 1, :] for n in range(6))

    def head_sum(x):
        return jnp.concatenate([_group_sum(x[:, s * W:(s + 1) * W], ones_bd)
                                for s in range(D // W)], axis=1)

    r, k, v = r_ref[...], k_ref[...], v_ref[...]
    z_w = w0 + jnp.dot(hw_ref[...], w2_ref[...], preferred_element_type=F32)
    lw = -DECAY_SCALE * _sigmoid(z_w)
    a_sig = _sigmoid(a0 + jnp.dot(ha_ref[...], a2_ref[...], preferred_element_type=F32))
    if has_vres:
        mix = _sigmoid(v0 + jnp.dot(hv_ref[...], v2_ref[...], preferred_element_type=F32))
        v = v + (vf_ref[...] - v) * mix
    vo_ref[...] = v.astype(BF16)
    kmod = k * (1.0 + (a_sig - 1.0) * k_a)
    bonus_ref[...] = head_sum(r * kmod * r_k) * v
    kk = k * k_k
    kk = kk * lax.rsqrt(jnp.maximum(head_sum(kk * kk), 1e-24))
    b = kk * a_sig

    L = _tri_sum(tri, lw)
    l_end = L[T - 1:T, :]
    for c in range(1, tm // T):
        l_end = jnp.where(row < c * T, l_end, L[(c + 1) * T - 1:(c + 1) * T, :])
    for c in range(tm // T):
        pend_ref[c] = jnp.broadcast_to(jnp.exp(L[(c + 1) * T - 1:(c + 1) * T, :]), (8, D))
    e_pos = jnp.exp(L)
    e_neg = 1.0 / e_pos
    to_end = jnp.exp(l_end - L)
    ao_ref[...] = (-kk * jnp.exp(L - lw)).astype(BF16)
    ro_ref[...] = (r * e_pos).astype(BF16)
    bo_ref[...] = (b * e_neg).astype(BF16)
    ko_ref[...] = (kmod * e_neg).astype(BF16)
    bdo_ref[...] = (b * to_end).astype(BF16)
    kdo_ref[...] = (kmod * to_end).astype(BF16)


def rwkv_pre(rkv, hw, ha, hv, w2, a2, v2, vecs, v_first, tm=128):
    _, S, D = rkv.shape
    tm = min(tm, S)
    assert tm % REC_CHUNK == 0
    has_vres = v_first is not None
    row = lambda i: (i, 0)
    full = lambda i: (0, 0)
    act = pl.BlockSpec((tm, D), row)
    rkv_spec = lambda n: pl.BlockSpec((None, tm, D), lambda i: (n, i, 0))
    lora = lambda a: pl.BlockSpec((tm, a.shape[1]), row)
    wspec = lambda a: pl.BlockSpec(a.shape, full)
    if has_vres:
        in_specs = [rkv_spec(0), rkv_spec(1), rkv_spec(2), lora(hw), lora(ha), lora(hv),
                    wspec(w2), wspec(a2), wspec(v2), wspec(vecs), rkv_spec(2)]
        args = [rkv, rkv, rkv, hw, ha, hv, w2, a2, v2, vecs, v_first]
    else:
        in_specs = [rkv_spec(0), rkv_spec(1), rkv_spec(2), lora(hw), lora(ha),
                    wspec(w2), wspec(a2), wspec(vecs)]
        args = [rkv, rkv, rkv, hw, ha, w2, a2, vecs]
    n_chunks = S // REC_CHUNK
    pend_spec = pl.BlockSpec((tm // REC_CHUNK, 8, D), lambda i: (i, 0, 0))
    bf_out = jax.ShapeDtypeStruct((S, D), BF16)
    f32_out = jax.ShapeDtypeStruct((S, D), F32)
    return pl.pallas_call(
        functools.partial(_pre_kernel, has_vres=has_vres),
        grid=(S // tm,),
        in_specs=in_specs,
        out_specs=[act] * 7 + [pend_spec, act],
        out_shape=[bf_out] * 7 + [jax.ShapeDtypeStruct((n_chunks, 8, D), F32), f32_out],
        compiler_params=_cparams("parallel"),
        name="rwkv_pre",
    )(*args)


def _rec_kernel(a_ref, r_ref, b_ref, k_ref, bd_ref, kd_ref, v_ref, pend_ref, y_ref, st_ref):
    T, W = REC_CHUNK, V7X_LANES
    D = a_ref.shape[-1]
    slabs = range(D // W)

    @pl.when(pl.program_id(0) == 0)
    def _():
        st_ref[...] = jnp.zeros_like(st_ref)

    lane = lax.broadcasted_iota(jnp.int32, (T, W), 1)
    head0 = lane < RWKV_HEAD
    t_row = lax.broadcasted_iota(jnp.int32, (T, W), 0)
    strict = (lane & (T - 1)) < t_row
    incl = (lane & (T - 1)) <= t_row
    same_head = (lax.broadcasted_iota(jnp.int32, (W, W), 0) // RWKV_HEAD
                 == lax.broadcasted_iota(jnp.int32, (W, W), 1) // RWKV_HEAD)

    def stacked(x):
        x = x.astype(F32)
        return jnp.concatenate([jnp.where(head0, x, 0.0), jnp.where(head0, 0.0, x)],
                               axis=0).astype(BF16)

    def tile(ref, p):
        return ref[:, p * W:(p + 1) * W]

    def mm(x, y):
        return jnp.dot(x.astype(BF16), y.astype(BF16), preferred_element_type=F32)

    def mm_nt(x, y):
        return lax.dot_general(x.astype(BF16), y.astype(BF16), (((1,), (1,)), ((), ())),
                               preferred_element_type=F32)

    def mm_tn(x, y):
        return lax.dot_general(x.astype(BF16), y.astype(BF16), (((0,), (0,)), ((), ())),
                               preferred_element_type=F32)

    ar = [jnp.concatenate([tile(a_ref, p), tile(r_ref, p)], axis=0) for p in slabs]
    gram = [mm_nt(ar[p], jnp.concatenate([stacked(tile(b_ref, p)), stacked(tile(k_ref, p))],
                                         axis=0)) for p in slabs]
    n_w = [jnp.where(strict, g[:T, :W], 0.0) for g in gram]
    a_rb = [jnp.where(incl, g[T:, :W], 0.0).astype(BF16) for g in gram]
    av = [mm(jnp.concatenate([jnp.where(strict, g[:T, W:], 0.0),
                              jnp.where(incl, g[T:, W:], 0.0)], axis=0),
             stacked(tile(v_ref, p))) for p, g in zip(slabs, gram)]

    s0 = [st_ref[p] for p in slabs]
    wr = [mm_nt(ar[p], s0[p]) for p in slabs]

    u_w = [wr[p][:T] + av[p][:T] for p in slabs]
    for _ in range(5):
        z = [mm(n_w[p], jnp.concatenate([stacked(n_w[p]), stacked(u_w[p])], axis=1))
             for p in slabs]
        n_w = [zp[:, :W] for zp in z]
        u_w = [u_w[p] + z[p][:, W:] for p in slabs]
    u_w = [u_w[p] + mm(n_w[p], stacked(u_w[p])) for p in slabs]
    for p in slabs:
        upd = mm_tn(jnp.concatenate([u_w[p].astype(BF16), tile(v_ref, p)], axis=0),
                    jnp.concatenate([tile(bd_ref, p), tile(kd_ref, p)], axis=0))
        st_ref[p] = s0[p] * tile(pend_ref, p)[0:1, :] + jnp.where(same_head, upd, 0.0)
    for p in slabs:
        y_ref[:, p * W:(p + 1) * W] = wr[p][T:] + mm(a_rb[p], stacked(u_w[p])) + av[p][T:]


def rwkv_recurrence(a, r, b, k, bd, kd, v, pend):
    S, D = a.shape
    T = REC_CHUNK
    act = pl.BlockSpec((T, D), lambda c: (c, 0))
    return pl.pallas_call(
        _rec_kernel,
        grid=(S // T,),
        in_specs=[act] * 7 + [pl.BlockSpec((None, 8, D), lambda c: (c, 0, 0))],
        out_specs=act,
        out_shape=jax.ShapeDtypeStruct((S, D), F32),
        scratch_shapes=[pltpu.VMEM((D // V7X_LANES, V7X_LANES, V7X_LANES), F32)],
        compiler_params=_cparams("arbitrary"),
        name="rwkv_recurrence",
    )(a, r, b, k, bd, kd, v, pend)


def _post_kernel(y_ref, bonus_ref, hg_ref, g2_ref, lnw_ref, lnb_ref, o_ref):
    W = V7X_LANES
    ones_bd = _head_sum_matrix(W, RWKV_HEAD)
    inv_n = 1.0 / RWKV_HEAD

    def head_sum(x):
        return jnp.concatenate([_group_sum(x[:, s * W:(s + 1) * W], ones_bd)
                                for s in range(x.shape[-1] // W)], axis=1)

    y = y_ref[...]
    d = y - head_sum(y) * inv_n
    var = head_sum(d * d) * inv_n
    yn = d * lax.rsqrt(var + LNX_EPS) * lnw_ref[...] + lnb_ref[...]
    gate = jnp.dot(hg_ref[...], g2_ref[...], preferred_element_type=F32)
    o_ref[...] = ((yn + bonus_ref[...]) * gate).astype(o_ref.dtype)


def rwkv_post(y, bonus, hg, g2, lnw, lnb, tm=256):
    S, D = y.shape
    tm = min(tm, S)
    act = pl.BlockSpec((tm, D), lambda i: (i, 0))
    vec = pl.BlockSpec((1, D), lambda i: (0, 0))
    return pl.pallas_call(
        _post_kernel,
        grid=(S // tm,),
        in_specs=[act, act, pl.BlockSpec((tm, hg.shape[1]), lambda i: (i, 0)),
                  pl.BlockSpec(g2.shape, lambda i: (0, 0)), vec, vec],
        out_specs=act,
        out_shape=jax.ShapeDtypeStruct((S, D), BF16),
        compiler_params=_cparams("parallel"),
        name="rwkv_post",
    )(y, bonus, hg, g2, lnw, lnb)


def _mlp_kernel(x_ref, g_ref, wu_ref, wd_ref, *rest, emit_x, has_post):
    if not has_post:
        acc_ref, xn_ref = rest
    elif emit_x:
        pg_ref, acc_ref, post_ref, xn_ref = rest
    else:
        pg_ref, post_ref, xn_ref, acc_ref = rest
    f = pl.program_id(1)

    @pl.when(f == 0)
    def _():
        x = x_ref[...]
        xn_ref[...] = _rms(x, g_ref[...]).astype(BF16)
        acc_ref[...] = x

    h = jnp.dot(xn_ref[...], wu_ref[...], preferred_element_type=F32)
    h = jnp.square(jnp.maximum(h, 0.0)).astype(BF16)
    acc_ref[...] += jnp.dot(h, wd_ref[...], preferred_element_type=F32)

    if has_post:
        @pl.when(f == pl.num_programs(1) - 1)
        def _():
            y = acc_ref[...]
            yn = y * lax.rsqrt(jnp.mean(y * y, axis=-1, keepdims=True) + RMS_EPS)
            for n in range(post_ref.shape[0]):
                post_ref[n] = (yn * pg_ref[n:n + 1, :]).astype(post_ref.dtype)


def mlp_block(x, gain, w_up, w_down, layer, post_gains=None, post_dtype=BF16, emit_x=True,
              tm=512, tf=1024):
    S, D = x.shape
    F = w_up.shape[-1]
    has_post = post_gains is not None
    assert has_post or emit_x
    tm, tf = min(tm, S), min(tf, F)
    row = pl.BlockSpec((tm, D), lambda i, f: (i, 0))
    in_specs = [row,
                pl.BlockSpec((1, D), lambda i, f: (0, 0)),
                pl.BlockSpec((None, D, tf), lambda i, f: (layer, 0, f)),
                pl.BlockSpec((None, tf, D), lambda i, f: (layer, f, 0))]
    args = [x, gain, w_up, w_down]
    out_specs = [row] if emit_x else []
    out_shape = [jax.ShapeDtypeStruct((S, D), F32)] if emit_x else []
    if has_post:
        G = post_gains.shape[0]
        in_specs.append(pl.BlockSpec((G, D), lambda i, f: (0, 0)))
        args.append(post_gains)
        out_specs.append(pl.BlockSpec((G, tm, D), lambda i, f: (0, i, 0)))
        out_shape.append(jax.ShapeDtypeStruct((G, S, D), post_dtype))
    outs = pl.pallas_call(
        functools.partial(_mlp_kernel, emit_x=emit_x, has_post=has_post),
        grid=(S // tm, F // tf),
        in_specs=in_specs,
        out_specs=out_specs,
        out_shape=out_shape,
        scratch_shapes=[pltpu.VMEM((tm, D), BF16)] + ([] if emit_x else [pltpu.VMEM((tm, D), F32)]),
        compiler_params=_cparams("parallel", "arbitrary"),
        name="mlp",
    )(*args)
    return outs if len(outs) > 1 else outs[0]


def _cumsum_kernel(x_ref, o_ref, carry_ref):
    @pl.when(pl.program_id(0) == 0)
    def _():
        carry_ref[...] = jnp.zeros_like(carry_ref)

    n = x_ref.shape[0]
    tri = (lax.broadcasted_iota(jnp.int32, (n, n), 1)
           <= lax.broadcasted_iota(jnp.int32, (n, n), 0)).astype(BF16)
    c = _tri_sum(tri, x_ref[...]) + carry_ref[0:1, :]
    o_ref[...] = c
    carry_ref[...] = jnp.broadcast_to(c[n - 1:n, :], carry_ref.shape)


def cumsum_rows(x, tc=256):
    S, W = x.shape
    tc = min(tc, S)
    return pl.pallas_call(
        _cumsum_kernel,
        grid=(S // tc,),
        in_specs=[pl.BlockSpec((tc, W), lambda i: (i, 0))],
        out_specs=pl.BlockSpec((tc, W), lambda i: (i, 0)),
        out_shape=jax.ShapeDtypeStruct((S, W), F32),
        scratch_shapes=[pltpu.VMEM((8, W), F32)],
        compiler_params=_cparams("arbitrary"),
        name="forget_cumsum",
    )(x)


FOX_ONES_ROWS = 16


def _fox_kernel(q_ref, k_ref, vt_ref, c_ref, o_ref, ck_ref, m_ref, acc_ref, s_ref, *, tq, tk):
    h = pl.program_id(0)
    qi = pl.program_id(1)
    nsub = tq // tk
    hd = k_ref.shape[1]
    S = k_ref.shape[0]

    @pl.when(qi == 0)
    def _():
        lane = lax.broadcasted_iota(jnp.int32, (tk, V7X_LANES), 1)
        for j in range(S // tk):
            col = jnp.sum(jnp.where(lane == h, c_ref[j * tk:(j + 1) * tk, :], 0.0),
                          axis=1, keepdims=True)
            ck_ref[j * tk:(j + 1) * tk, :] = jnp.broadcast_to(col * LOG2_E, (tk, V7X_LANES))

    m_ref[...] = jnp.full_like(m_ref, MASK_VALUE)
    acc_ref[...] = jnp.zeros_like(acc_ref)
    q = q_ref[...]
    ones_rows = jnp.ones((FOX_ONES_ROWS, tk), BF16)

    def scores(kj, slot, lo=0):
        start = pl.multiple_of(kj * tk, tk)
        s_ref[slot, :, lo:] = lax.dot_general(
            k_ref[pl.ds(start, tk), :], q[lo:], (((1,), (1,)), ((), ())),
            preferred_element_type=F32)

    def update(kj, slot, lo=0, hi=tq, diagonal=False):
        start = pl.multiple_of(kj * tk, tk)
        ck = ck_ref[pl.ds(start, tk), :]
        s = s_ref[slot, :, lo:hi] - jnp.concatenate([ck] * ((hi - lo) // V7X_LANES), axis=1)
        if diagonal:
            key = lax.broadcasted_iota(jnp.int32, (tk, tk), 0)
            qry = lax.broadcasted_iota(jnp.int32, (tk, tk), 1)
            s = jnp.where(qry >= key, s, MASK_VALUE)
        m_prev = m_ref[0:1, lo:hi]
        m_new = jnp.maximum(m_prev, jnp.max(s, axis=0, keepdims=True))
        alpha = jnp.exp2(m_prev - m_new)
        p = jnp.exp2(s - m_new).astype(BF16)
        vt = jnp.concatenate([vt_ref[kj], ones_rows], axis=0)
        acc_ref[:, lo:hi] = (alpha * acc_ref[:, lo:hi]
                             + jnp.dot(vt, p, preferred_element_type=F32))
        m_ref[0:1, lo:hi] = m_new

    assert nsub == 2
    scores(0, 0)

    def block_pair(base):
        scores(base + 1, 1)
        update(base, 0)
        scores(base + 2, 0)
        update(base + 1, 1)

    def four_blocks(i, carry):
        block_pair(4 * i)
        block_pair(4 * i + 2)
        return carry

    lax.fori_loop(0, qi // 2, four_blocks, 0)

    @pl.when(qi % 2 == 1)
    def _():
        block_pair(2 * qi - 2)
    scores(2 * qi + 1, 1, lo=tk)
    update(2 * qi, 0, 0, tk, diagonal=True)
    update(2 * qi, 0, tk, tq)
    update(2 * qi + 1, 1, tk, tq, diagonal=True)
    out_t = acc_ref[0:hd, :] / acc_ref[hd:hd + 1, :]
    o_ref[...] = out_t.T.astype(o_ref.dtype)


def fox_attention(q, k, vt, c, n_heads, tq=1024, tk=512):
    S = q.shape[0]
    hd = FOX_HEAD
    tq = min(tq, S)
    assert tq % tk == 0 and vt.shape[-1] == tk
    return pl.pallas_call(
        functools.partial(_fox_kernel, tq=tq, tk=tk),
        grid=(n_heads, S // tq),
        in_specs=[pl.BlockSpec((tq, hd), lambda h, i: (i, h)),
                  pl.BlockSpec((S, hd), lambda h, i: (0, h)),
                  pl.BlockSpec((None, S // tk, hd, tk), lambda h, i: (h, 0, 0, 0)),
                  pl.BlockSpec((S, V7X_LANES), lambda h, i: (0, 0))],
        out_specs=pl.BlockSpec((tq, hd), lambda h, i: (i, h)),
        out_shape=jax.ShapeDtypeStruct((S, n_heads * hd), BF16),
        scratch_shapes=[pltpu.VMEM((S, V7X_LANES), F32),
                        pltpu.VMEM((8, tq), F32),
                        pltpu.VMEM((hd + FOX_ONES_ROWS, tq), F32),
                        pltpu.VMEM((2, tk, tq), F32)],
        compiler_params=_cparams("parallel", "arbitrary"),
        name="fox_attention",
    )(q, k, vt, c)


def _pad_cols(w, n):
    return jnp.pad(w, ((0, 0), (0, n - w.shape[1])))


def _pad_rows(w, n):
    return jnp.pad(w, ((0, n - w.shape[0]), (0, 0)))


def kernel(x, mix_norm, ffn_norm, final_norm, rwkv_x_mix, rwkv_w_rkv, rwkv_w0, rwkv_w1, rwkv_w2, rwkv_a0, rwkv_a1, rwkv_a2, rwkv_v0, rwkv_v1, rwkv_v2, rwkv_g1, rwkv_g2, rwkv_k_k, rwkv_k_a, rwkv_r_k, rwkv_lnx_w, rwkv_lnx_b, rwkv_w_o, kv_norm, w_kvf, b_f, fox_w_q, fox_w_o, mlp_w_up, mlp_w_down):
    B, S, D = x.shape
    assert B == 1
    n_a = rwkv_x_mix.shape[0]
    n_b = fox_w_q.shape[0]
    assert n_a >= 1 and n_b >= 1
    n_fox_heads = D // FOX_HEAD
    lora_pad = V7X_LANES
    fox_tk = min(512, S)
    xs = x.reshape(S, D)
    bf = lambda w: w.astype(BF16)
    w_up_bf, w_down_bf = bf(mlp_w_up), bf(mlp_w_down)

    v_first = None
    for i in range(n_a):
        w1 = bf(_pad_cols(rwkv_w1[i], lora_pad))
        a1 = bf(_pad_cols(rwkv_a1[i], lora_pad))
        v1 = bf(_pad_cols(rwkv_v1[i - 1], lora_pad)) if i > 0 else None
        mixed, hw, ha, hg, *hv = rwkv_mix(xs, mix_norm[i:i + 1], rwkv_x_mix[i], w1, a1,
                                          bf(rwkv_g1[i]), v1)
        rkv = matmul(mixed, rwkv_w_rkv, w_idx=(i,), name="rkv_proj")
        w2 = bf(_pad_rows(rwkv_w2[i], lora_pad))
        a2 = bf(_pad_rows(rwkv_a2[i], lora_pad))
        zeros = jnp.zeros((D,), F32)
        r_k = rwkv_r_k[i].reshape(D)
        if i == 0:
            vecs = jnp.stack([rwkv_w0[i], rwkv_a0[i], zeros, rwkv_k_k[i], rwkv_k_a[i], r_k,
                              zeros, zeros])
            pre = rwkv_pre(rkv, hw, ha, None, w2, a2, None, vecs, None)
            v_first = rkv
        else:
            vecs = jnp.stack([rwkv_w0[i], rwkv_a0[i], rwkv_v0[i - 1], rwkv_k_k[i], rwkv_k_a[i],
                              r_k, zeros, zeros])
            v2 = bf(_pad_rows(rwkv_v2[i - 1], lora_pad))
            pre = rwkv_pre(rkv, hw, ha, hv[0], w2, a2, v2, vecs, v_first)
        *rec_in, bonus = pre
        y = rwkv_recurrence(*rec_in)
        y = rwkv_post(y, bonus, hg, bf(rwkv_g2[i]), rwkv_lnx_w[i:i + 1], rwkv_lnx_b[i:i + 1])
        xs = matmul(y, rwkv_w_o, w_idx=(i,), res=xs, name="rwkv_out")
        if i + 1 < n_a:
            xs = mlp_block(xs, ffn_norm[i:i + 1], w_up_bf, w_down_bf, i)
        else:
            xs, normed = mlp_block(xs, ffn_norm[i:i + 1], w_up_bf, w_down_bf, i,
                                   post_gains=jnp.stack([mix_norm[n_a], kv_norm]))

    k_sh = vt = c = out = None
    for j in range(n_b):
        layer = n_a + j
        if j == 0:
            k_sh = matmul(normed, w_kvf, x_idx=1, n_out=D, out_dtype=BF16, name="k_proj")
            vt = matmul_head_transposed(normed, 1, w_kvf, D, n_fox_heads, FOX_HEAD, fox_tk)
            b_pad = jnp.pad(b_f, (0, V7X_LANES - n_fox_heads)).reshape(1, V7X_LANES)
            log_f = matmul(normed, bf(_pad_cols(w_kvf[:, 2 * D:], V7X_LANES)), x_idx=1,
                           bias=b_pad, act=_log_sigmoid, name="forget_proj")
            c = cumsum_rows(log_f)
        q = matmul(normed, fox_w_q, w_idx=(j,), x_idx=0, scale=LOG2_E * FOX_HEAD ** -0.5,
                   out_dtype=BF16, name="q_proj")
        o = fox_attention(q, k_sh, vt, c, n_fox_heads, tk=fox_tk)
        xs = matmul(o, fox_w_o, w_idx=(j,), res=xs, name="fox_out")
        if j + 1 < n_b:
            xs, normed = mlp_block(xs, ffn_norm[layer:layer + 1], w_up_bf, w_down_bf, layer,
                                   post_gains=mix_norm[layer + 1:layer + 2])
        else:
            out = mlp_block(xs, ffn_norm[layer:layer + 1], w_up_bf, w_down_bf, layer,
                            post_gains=final_norm.reshape(1, D), post_dtype=F32, emit_x=False)
    return out.reshape(B, S, D)
```

```python
import functools
import math

import jax
import jax.numpy as jnp
from jax import lax
from jax.experimental import pallas as pl
from jax.experimental.pallas import tpu as pltpu

F32 = jnp.float32
BF16 = jnp.bfloat16

RWKV_HEAD = 64
FOX_HEAD = 128
RMS_EPS = 1e-6
LNX_EPS = 1e-5 * RWKV_HEAD
V7X_LANES = 128
REC_CHUNK = 64
V7X_VMEM_LIMIT = 56 * 1024 * 1024
MASK_VALUE = -1e30
LOG2_E = math.log2(math.e)
DECAY_SCALE = math.exp(-0.5)


def _cparams(*sem):
    return pltpu.CompilerParams(dimension_semantics=sem, vmem_limit_bytes=V7X_VMEM_LIMIT)


def _sigmoid(z):
    return 1.0 / (1.0 + jnp.exp(-z))


def _log_sigmoid(z):
    return jnp.minimum(z, 0.0) - jnp.log(1.0 + jnp.exp(-jnp.abs(z)))


def _rms(x, gain):
    return x * lax.rsqrt(jnp.mean(x * x, axis=-1, keepdims=True) + RMS_EPS) * gain


def _head_sum_matrix(n, head):
    r = lax.broadcasted_iota(jnp.int32, (n, n), 0) // head
    c = lax.broadcasted_iota(jnp.int32, (n, n), 1) // head
    return (r == c).astype(BF16)


def _group_sum(x, ones_bd):
    return jnp.dot(x.astype(BF16), ones_bd, preferred_element_type=F32)


def _tri_sum(tri, x):
    hi = x.astype(BF16)
    r1 = x - hi.astype(F32)
    mid = r1.astype(BF16)
    lo = (r1 - mid.astype(F32)).astype(BF16)
    return (jnp.dot(tri, hi, preferred_element_type=F32)
            + jnp.dot(tri, mid, preferred_element_type=F32)
            + jnp.dot(tri, lo, preferred_element_type=F32))


def _mix_kernel(x_ref, halo_ref, g_ref, mix_ref, w1_ref, a1_ref, g1_ref, *rest, has_vres):
    if has_vres:
        v1_ref, o_ref, hw_ref, ha_ref, hg_ref, hv_ref = rest
    else:
        o_ref, hw_ref, ha_ref, hg_ref = rest
    i = pl.program_id(0)
    gain = g_ref[...]
    h = _rms(x_ref[...], gain)
    hh = _rms(halo_ref[...], gain)
    last = jnp.where(i == 0, 0.0, hh[7:8, :])
    row = lax.broadcasted_iota(jnp.int32, h.shape, 0)
    prev = jnp.where(row == 0, last, pltpu.roll(h, 1, 0))
    d = prev - h
    mixed = lambda n: (h + d * mix_ref[n:n + 1, :]).astype(BF16)
    lora = lambda n, w_ref: jnp.dot(mixed(n), w_ref[...], preferred_element_type=F32)
    xv = mixed(2)
    o_ref[0] = mixed(0)
    o_ref[1] = mixed(1)
    o_ref[2] = xv
    hw_ref[...] = jnp.tanh(lora(3, w1_ref)).astype(BF16)
    ha_ref[...] = lora(4, a1_ref).astype(BF16)
    hg_ref[...] = _sigmoid(lora(5, g1_ref)).astype(BF16)
    if has_vres:
        hv_ref[...] = jnp.dot(xv, v1_ref[...], preferred_element_type=F32).astype(BF16)


def rwkv_mix(x, gain, x_mix, w1, a1, g1, v1, tm=256):
    S, D = x.shape
    tm = min(tm, S)
    hb = tm // 8
    has_vres = v1 is not None
    row = lambda i: (i, 0)
    full = lambda i: (0, 0)
    loras = [w1, a1, g1] + ([v1] if has_vres else [])
    return pl.pallas_call(
        functools.partial(_mix_kernel, has_vres=has_vres),
        grid=(S // tm,),
        in_specs=[pl.BlockSpec((tm, D), row),
                  pl.BlockSpec((8, D), lambda i: (jnp.maximum(i * hb - 1, 0), 0)),
                  pl.BlockSpec((1, D), full),
                  pl.BlockSpec(x_mix.shape, full)]
                 + [pl.BlockSpec(w.shape, full) for w in loras],
        out_specs=[pl.BlockSpec((3, tm, D), lambda i: (0, i, 0))]
                  + [pl.BlockSpec((tm, w.shape[1]), row) for w in loras],
        out_shape=[jax.ShapeDtypeStruct((3, S, D), BF16)]
                  + [jax.ShapeDtypeStruct((S, w.shape[1]), BF16) for w in loras],
        compiler_params=_cparams("parallel"),
        name="rwkv_mix",
    )(x, x, gain, x_mix, *loras)


def _mm_kernel(x_ref, w_ref, *rest, act, scale, has_bias, has_res, cast_w, row_axis):
    if cast_w:
        o_ref, wbf_ref = rest[-2], rest[-1]

        @pl.when(pl.program_id(row_axis) == 0)
        def _():
            wbf_ref[...] = w_ref[...].astype(BF16)
        w = wbf_ref[...]
    else:
        o_ref = rest[-1]
        w = w_ref[...]
    acc = jnp.dot(x_ref[...], w, preferred_element_type=F32)
    if scale != 1.0:
        acc = acc * scale
    if has_bias:
        acc = acc + rest[0][...]
    if act is not None:
        acc = act(acc)
    if has_res:
        acc = acc + rest[1 if has_bias else 0][...]
    o_ref[...] = acc.astype(o_ref.dtype)


def matmul(x, w, *, x_idx=None, w_idx=(), n_out=None, act=None, scale=1.0, bias=None, res=None,
           out_dtype=F32, tm=1024, tn=1024, name="matmul"):
    M, K = x.shape[-2:]
    N = w.shape[-1] if n_out is None else n_out
    batched = w.ndim - len(w_idx) == 3
    tm, tn = min(tm, M), min(tn, N)
    cast_w = w.dtype != BF16
    lead = (None,) * len(w_idx)
    if batched:
        nb = w.shape[len(w_idx)]
        grid = (nb, N // tn, M // tm)
        x_spec = pl.BlockSpec((None, tm, K), lambda b, j, i: (b, i, 0))
        w_spec = pl.BlockSpec(lead + (None, K, tn), lambda b, j, i: w_idx + (b, 0, j))
        o_spec = pl.BlockSpec((None, tm, tn), lambda b, j, i: (b, i, j))
        out_shape = jax.ShapeDtypeStruct((nb, M, N), out_dtype)
        sem = ("parallel", "parallel", "arbitrary")
    else:
        grid = (N // tn, M // tm)
        if x_idx is None:
            x_spec = pl.BlockSpec((tm, K), lambda j, i: (i, 0))
        else:
            x_spec = pl.BlockSpec((None, tm, K), lambda j, i: (x_idx, i, 0))
        w_spec = pl.BlockSpec(lead + (K, tn), lambda j, i: w_idx + (0, j))
        o_spec = pl.BlockSpec((tm, tn), lambda j, i: (i, j))
        out_shape = jax.ShapeDtypeStruct((M, N), out_dtype)
        sem = ("parallel", "arbitrary")
    kern = functools.partial(_mm_kernel, act=act, scale=scale, has_bias=bias is not None,
                             has_res=res is not None, cast_w=cast_w, row_axis=len(grid) - 1)
    in_specs = [x_spec, w_spec]
    args = [x, w]
    if bias is not None:
        assert not batched
        in_specs.append(pl.BlockSpec((1, tn), lambda j, i: (0, j)))
        args.append(bias)
    if res is not None:
        assert not batched
        in_specs.append(o_spec)
        args.append(res)
    return pl.pallas_call(
        kern, grid=grid, in_specs=in_specs, out_specs=o_spec, out_shape=out_shape,
        scratch_shapes=[pltpu.VMEM((K, tn), BF16)] if cast_w else [],
        compiler_params=_cparams(*sem), name=name,
    )(*args)


def _proj_t_kernel(x_ref, w_ref, o_ref, wbf_ref):
    @pl.when(pl.program_id(1) == 0)
    def _():
        wbf_ref[...] = w_ref[...].astype(BF16)

    acc = jnp.dot(x_ref[...], wbf_ref[...], preferred_element_type=F32)
    n_heads, n_blocks, hd, tk = o_ref.shape
    for h in range(n_heads):
        for b in range(n_blocks):
            tile = acc[b * tk:(b + 1) * tk, h * hd:(h + 1) * hd]
            o_ref[h, b] = tile.T.astype(o_ref.dtype)


def matmul_head_transposed(x, x_idx, w, col0, n_heads, hd, tk, tm=1024, tn=1024):
    _, M, K = x.shape
    tm = min(tm, M)
    assert col0 % tn == 0 and (n_heads * hd) % tn == 0 and tn % hd == 0 and tm % tk == 0
    return pl.pallas_call(
        _proj_t_kernel,
        grid=(n_heads * hd // tn, M // tm),
        in_specs=[pl.BlockSpec((None, tm, K), lambda j, i: (x_idx, i, 0)),
                  pl.BlockSpec((K, tn), lambda j, i: (0, col0 // tn + j))],
        out_specs=pl.BlockSpec((tn // hd, tm // tk, hd, tk), lambda j, i: (j, i, 0, 0)),
        out_shape=jax.ShapeDtypeStruct((n_heads, M // tk, hd, tk), BF16),
        scratch_shapes=[pltpu.VMEM((K, tn), BF16)],
        compiler_params=_cparams("parallel", "arbitrary"),
        name="proj_head_transposed",
    )(x, w)


def _pre_kernel(*refs, has_vres):
    if has_vres:
        (r_ref, k_ref, v_ref, hw_ref, ha_ref, hv_ref, w2_ref, a2_ref, v2_ref,
         vec_ref, vf_ref, *outs) = refs
    else:
        (r_ref, k_ref, v_ref, hw_ref, ha_ref, w2_ref, a2_ref, vec_ref, *outs) = refs
    ao_ref, ro_ref, bo_ref, ko_ref, bdo_ref, kdo_ref, vo_ref, pend_ref, bonus_ref = outs
    tm, D = k_ref.shape
    T, W = REC_CHUNK, V7X_LANES
    ri = lax.broadcasted_iota(jnp.int32, (tm, tm), 0)
    ci = lax.broadcasted_iota(jnp.int32, (tm, tm), 1)
    tri = jnp.logical_and(ci <= ri, ci // T == ri // T).astype(BF16)
    row = lax.broadcasted_iota(jnp.int32, (tm, D), 0)
    ones_bd = _head_sum_matrix(W, RWKV_HEAD)
    w0, a0, v0, k_k, k_a, r_k = (vec_ref[n:n + 1, :] for n in range(6))

    def head_sum(x):
        return jnp.concatenate([_group_sum(x[:, s * W:(s + 1) * W], ones_bd)
                                for s in range(D // W)], axis=1)

    r, k, v = r_ref[...], k_ref[...], v_ref[...]
    z_w = w0 + jnp.dot(hw_ref[...], w2_ref[...], preferred_element_type=F32)
    lw = -DECAY_SCALE * _sigmoid(z_w)
    a_sig = _sigmoid(a0 + jnp.dot(ha_ref[...], a2_ref[...], preferred_element_type=F32))
    if has_vres:
        mix = _sigmoid(v0 + jnp.dot(hv_ref[...], v2_ref[...], preferred_element_type=F32))
        v = v + (vf_ref[...] - v) * mix
    vo_ref[...] = v.astype(BF16)
    kmod = k * (1.0 + (a_sig - 1.0) * k_a)
    bonus_ref[...] = head_sum(r * kmod * r_k) * v
    kk = k * k_k
    kk = kk * lax.rsqrt(jnp.maximum(head_sum(kk * kk), 1e-24))
    b = kk * a_sig

    L = _tri_sum(tri, lw)
    l_end = L[T - 1:T, :]
    for c in range(1, tm // T):
        l_end = jnp.where(row < c * T, l_end, L[(c + 1) * T - 1:(c + 1) * T, :])
    for c in range(tm // T):
        pend_ref[c] = jnp.broadcast_to(jnp.exp(L[(c + 1) * T - 1:(c + 1) * T, :]), (8, D))
    e_pos = jnp.exp(L)
    e_neg = 1.0 / e_pos
    to_end = jnp.exp(l_end - L)
    ao_ref[...] = (-kk * jnp.exp(L - lw)).astype(BF16)
    ro_ref[...] = (r * e_pos).astype(BF16)
    bo_ref[...] = (b * e_neg).astype(BF16)
    ko_ref[...] = (kmod * e_neg).astype(BF16)
    bdo_ref[...] = (b * to_end).astype(BF16)
    kdo_ref[...] = (kmod * to_end).astype(BF16)


def rwkv_pre(rkv, hw, ha, hv, w2, a2, v2, vecs, v_first, tm=256):
    _, S, D = rkv.shape
    tm = min(tm, S)
    assert tm % REC_CHUNK == 0
    has_vres = v_first is not None
    row = lambda i: (i, 0)
    full = lambda i: (0, 0)
    act = pl.BlockSpec((tm, D), row)
    rkv_spec = lambda n: pl.BlockSpec((None, tm, D), lambda i: (n, i, 0))
    lora = lambda a: pl.BlockSpec((tm, a.shape[1]), row)
    wspec = lambda a: pl.BlockSpec(a.shape, full)
    if has_vres:
        in_specs = [rkv_spec(0), rkv_spec(1), rkv_spec(2), lora(hw), lora(ha), lora(hv),
                    wspec(w2), wspec(a2), wspec(v2), wspec(vecs), rkv_spec(2)]
        args = [rkv, rkv, rkv, hw, ha, hv, w2, a2, v2, vecs, v_first]
    else:
        in_specs = [rkv_spec(0), rkv_spec(1), rkv_spec(2), lora(hw), lora(ha),
                    wspec(w2), wspec(a2), wspec(vecs)]
        args = [rkv, rkv, rkv, hw, ha, w2, a2, vecs]
    n_chunks = S // REC_CHUNK
    pend_spec = pl.BlockSpec((tm // REC_CHUNK, 8, D), lambda i: (i, 0, 0))
    bf_out = jax.ShapeDtypeStruct((S, D), BF16)
    f32_out = jax.ShapeDtypeStruct((S, D), F32)
    return pl.pallas_call(
        functools.partial(_pre_kernel, has_vres=has_vres),
        grid=(S // tm,),
        in_specs=in_specs,
        out_specs=[act] * 7 + [pend_spec, act],
        out_shape=[bf_out] * 7 + [jax.ShapeDtypeStruct((n_chunks, 8, D), F32), f32_out],
        compiler_params=_cparams("parallel"),
        name="rwkv_pre",
    )(*args)


def _rec_kernel(a_ref, r_ref, b_ref, k_ref, bd_ref, kd_ref, v_ref, pend_ref, y_ref, st_ref):
    T, W = REC_CHUNK, V7X_LANES
    D = a_ref.shape[-1]
    slabs = range(D // W)

    @pl.when(pl.program_id(0) == 0)
    def _():
        st_ref[...] = jnp.zeros_like(st_ref)

    lane = lax.broadcasted_iota(jnp.int32, (T, W), 1)
    head0 = lane < RWKV_HEAD
    t_row = lax.broadcasted_iota(jnp.int32, (T, W), 0)
    strict = (lane & (T - 1)) < t_row
    incl = (lane & (T - 1)) <= t_row
    same_head = (lax.broadcasted_iota(jnp.int32, (W, W), 0) // RWKV_HEAD
                 == lax.broadcasted_iota(jnp.int32, (W, W), 1) // RWKV_HEAD)

    def stacked(x):
        x = x.astype(F32)
        return jnp.concatenate([jnp.where(head0, x, 0.0), jnp.where(head0, 0.0, x)],
                               axis=0).astype(BF16)

    def tile(ref, p):
        return ref[:, p * W:(p + 1) * W]

    def mm(x, y):
        return jnp.dot(x.astype(BF16), y.astype(BF16), preferred_element_type=F32)

    def mm_nt(x, y):
        return lax.dot_general(x.astype(BF16), y.astype(BF16), (((1,), (1,)), ((), ())),
                               preferred_element_type=F32)

    def mm_tn(x, y):
        return lax.dot_general(x.astype(BF16), y.astype(BF16), (((0,), (0,)), ((), ())),
                               preferred_element_type=F32)

    ar = [jnp.concatenate([tile(a_ref, p), tile(r_ref, p)], axis=0) for p in slabs]
    gram = [mm_nt(ar[p], jnp.concatenate([stacked(tile(b_ref, p)), stacked(tile(k_ref, p))],
                                         axis=0)) for p in slabs]
    n_w = [jnp.where(strict, g[:T, :W], 0.0) for g in gram]
    a_rb = [jnp.where(incl, g[T:, :W], 0.0).astype(BF16) for g in gram]
    av = [mm(jnp.concatenate([jnp.where(strict, g[:T, W:], 0.0),
                              jnp.where(incl, g[T:, W:], 0.0)], axis=0),
             stacked(tile(v_ref, p))) for p, g in zip(slabs, gram)]

    s0 = [st_ref[p] for p in slabs]
    wr = [mm_nt(ar[p], s0[p]) for p in slabs]

    u_w = [wr[p][:T] + av[p][:T] for p in slabs]
    for _ in range(5):
        z = [mm(n_w[p], jnp.concatenate([stacked(n_w[p]), stacked(u_w[p])], axis=1))
             for p in slabs]
        n_w = [zp[:, :W] for zp in z]
        u_w = [u_w[p] + z[p][:, W:] for p in slabs]
    u_w = [u_w[p] + mm(n_w[p], stacked(u_w[p])) for p in slabs]
    for p in slabs:
        upd = mm_tn(jnp.concatenate([u_w[p].astype(BF16), tile(v_ref, p)], axis=0),
                    jnp.concatenate([tile(bd_ref, p), tile(kd_ref, p)], axis=0))
        st_ref[p] = s0[p] * tile(pend_ref, p)[0:1, :] + jnp.where(same_head, upd, 0.0)
    for p in slabs:
        y_ref[:, p * W:(p + 1) * W] = wr[p][T:] + mm(a_rb[p], stacked(u_w[p])) + av[p][T:]


def rwkv_recurrence(a, r, b, k, bd, kd, v, pend):
    S, D = a.shape
    T = REC_CHUNK
    act = pl.BlockSpec((T, D), lambda c: (c, 0))
    return pl.pallas_call(
        _rec_kernel,
        grid=(S // T,),
        in_specs=[act] * 7 + [pl.BlockSpec((None, 8, D), lambda c: (c, 0, 0))],
        out_specs=act,
        out_shape=jax.ShapeDtypeStruct((S, D), F32),
        scratch_shapes=[pltpu.VMEM((D // V7X_LANES, V7X_LANES, V7X_LANES), F32)],
        compiler_params=_cparams("arbitrary"),
        name="rwkv_recurrence",
    )(a, r, b, k, bd, kd, v, pend)


def _post_kernel(y_ref, bonus_ref, hg_ref, g2_ref, lnw_ref, lnb_ref, o_ref):
    W = V7X_LANES
    ones_bd = _head_sum_matrix(W, RWKV_HEAD)
    inv_n = 1.0 / RWKV_HEAD

    def head_sum(x):
        return jnp.concatenate([_group_sum(x[:, s * W:(s + 1) * W], ones_bd)
                                for s in range(x.shape[-1] // W)], axis=1)

    y = y_ref[...]
    d = y - head_sum(y) * inv_n
    var = head_sum(d * d) * inv_n
    yn = d * lax.rsqrt(var + LNX_EPS) * lnw_ref[...] + lnb_ref[...]
    gate = jnp.dot(hg_ref[...], g2_ref[...], preferred_element_type=F32)
    o_ref[...] = ((yn + bonus_ref[...]) * gate).astype(o_ref.dtype)


def rwkv_post(y, bonus, hg, g2, lnw, lnb, tm=512):
    S, D = y.shape
    tm = min(tm, S)
    act = pl.BlockSpec((tm, D), lambda i: (i, 0))
    vec = pl.BlockSpec((1, D), lambda i: (0, 0))
    return pl.pallas_call(
        _post_kernel,
        grid=(S // tm,),
        in_specs=[act, act, pl.BlockSpec((tm, hg.shape[1]), lambda i: (i, 0)),
                  pl.BlockSpec(g2.shape, lambda i: (0, 0)), vec, vec],
        out_specs=act,
        out_shape=jax.ShapeDtypeStruct((S, D), BF16),
        compiler_params=_cparams("parallel"),
        name="rwkv_post",
    )(y, bonus, hg, g2, lnw, lnb)


def _mlp_kernel(x_ref, g_ref, wu_ref, wd_ref, *rest, emit_x, has_post):
    if not has_post:
        acc_ref, xn_ref = rest
    elif emit_x:
        pg_ref, acc_ref, post_ref, xn_ref = rest
    else:
        pg_ref, post_ref, xn_ref, acc_ref = rest
    f = pl.program_id(1)

    @pl.when(f == 0)
    def _():
        x = x_ref[...]
        xn_ref[...] = _rms(x, g_ref[...]).astype(BF16)
        acc_ref[...] = x

    h = jnp.dot(xn_ref[...], wu_ref[...], preferred_element_type=F32)
    h = jnp.square(jnp.maximum(h, 0.0)).astype(BF16)
    acc_ref[...] += jnp.dot(h, wd_ref[...], preferred_element_type=F32)

    if has_post:
        @pl.when(f == pl.num_programs(1) - 1)
        def _():
            y = acc_ref[...]
            yn = y * lax.rsqrt(jnp.mean(y * y, axis=-1, keepdims=True) + RMS_EPS)
            for n in range(post_ref.shape[0]):
                post_ref[n] = (yn * pg_ref[n:n + 1, :]).astype(post_ref.dtype)


def mlp_block(x, gain, w_up, w_down, layer, post_gains=None, post_dtype=BF16, emit_x=True,
              tm=512, tf=1024):
    S, D = x.shape
    F = w_up.shape[-1]
    has_post = post_gains is not None
    assert has_post or emit_x
    tm, tf = min(tm, S), min(tf, F)
    row = pl.BlockSpec((tm, D), lambda i, f: (i, 0))
    in_specs = [row,
                pl.BlockSpec((1, D), lambda i, f: (0, 0)),
                pl.BlockSpec((None, D, tf), lambda i, f: (layer, 0, f)),
                pl.BlockSpec((None, tf, D), lambda i, f: (layer, f, 0))]
    args = [x, gain, w_up, w_down]
    out_specs = [row] if emit_x else []
    out_shape = [jax.ShapeDtypeStruct((S, D), F32)] if emit_x else []
    if has_post:
        G = post_gains.shape[0]
        in_specs.append(pl.BlockSpec((G, D), lambda i, f: (0, 0)))
        args.append(post_gains)
        out_specs.append(pl.BlockSpec((G, tm, D), lambda i, f: (0, i, 0)))
        out_shape.append(jax.ShapeDtypeStruct((G, S, D), post_dtype))
    outs = pl.pallas_call(
        functools.partial(_mlp_kernel, emit_x=emit_x, has_post=has_post),
        grid=(S // tm, F // tf),
        in_specs=in_specs,
        out_specs=out_specs,
        out_shape=out_shape,
        scratch_shapes=[pltpu.VMEM((tm, D), BF16)] + ([] if emit_x else [pltpu.VMEM((tm, D), F32)]),
        compiler_params=_cparams("parallel", "arbitrary"),
        name="mlp",
    )(*args)
    return outs if len(outs) > 1 else outs[0]


def _cumsum_kernel(x_ref, o_ref, carry_ref):
    @pl.when(pl.program_id(0) == 0)
    def _():
        carry_ref[...] = jnp.zeros_like(carry_ref)

    n = x_ref.shape[0]
    tri = (lax.broadcasted_iota(jnp.int32, (n, n), 1)
           <= lax.broadcasted_iota(jnp.int32, (n, n), 0)).astype(BF16)
    c = _tri_sum(tri, x_ref[...]) + carry_ref[0:1, :]
    o_ref[...] = c
    carry_ref[...] = jnp.broadcast_to(c[n - 1:n, :], carry_ref.shape)


def cumsum_rows(x, tc=256):
    S, W = x.shape
    tc = min(tc, S)
    return pl.pallas_call(
        _cumsum_kernel,
        grid=(S // tc,),
        in_specs=[pl.BlockSpec((tc, W), lambda i: (i, 0))],
        out_specs=pl.BlockSpec((tc, W), lambda i: (i, 0)),
        out_shape=jax.ShapeDtypeStruct((S, W), F32),
        scratch_shapes=[pltpu.VMEM((8, W), F32)],
        compiler_params=_cparams("arbitrary"),
        name="forget_cumsum",
    )(x)


FOX_ONES_ROWS = 16


def _fox_kernel(q_ref, k_ref, vt_ref, c_ref, o_ref, ck_ref, m_ref, acc_ref, s_ref, *, tq, tk):
    h = pl.program_id(0)
    qi = pl.program_id(1)
    nsub = tq // tk
    hd = k_ref.shape[1]
    S = k_ref.shape[0]

    @pl.when(qi == 0)
    def _():
        lane = lax.broadcasted_iota(jnp.int32, (tk, V7X_LANES), 1)
        for j in range(S // tk):
            col = jnp.sum(jnp.where(lane == h, c_ref[j * tk:(j + 1) * tk, :], 0.0),
                          axis=1, keepdims=True)
            ck_ref[j * tk:(j + 1) * tk, :] = jnp.broadcast_to(col * LOG2_E, (tk, V7X_LANES))

    m_ref[...] = jnp.full_like(m_ref, MASK_VALUE)
    acc_ref[...] = jnp.zeros_like(acc_ref)
    q = q_ref[...]
    ones_rows = jnp.ones((FOX_ONES_ROWS, tk), BF16)

    def scores(kj, slot, lo=0):
        start = pl.multiple_of(kj * tk, tk)
        s_ref[slot, :, lo:] = lax.dot_general(
            k_ref[pl.ds(start, tk), :], q[lo:], (((1,), (1,)), ((), ())),
            preferred_element_type=F32)

    def update(kj, slot, lo=0, hi=tq, diagonal=False):
        start = pl.multiple_of(kj * tk, tk)
        ck = ck_ref[pl.ds(start, tk), :]
        s = s_ref[slot, :, lo:hi] - jnp.concatenate([ck] * ((hi - lo) // V7X_LANES), axis=1)
        if diagonal:
            key = lax.broadcasted_iota(jnp.int32, (tk, tk), 0)
            qry = lax.broadcasted_iota(jnp.int32, (tk, tk), 1)
            s = jnp.where(qry >= key, s, MASK_VALUE)
        m_prev = m_ref[0:1, lo:hi]
        m_new = jnp.maximum(m_prev, jnp.max(s, axis=0, keepdims=True))
        alpha = jnp.exp2(m_prev - m_new)
        p = jnp.exp2(s - m_new).astype(BF16)
        vt = jnp.concatenate([vt_ref[kj], ones_rows], axis=0)
        acc_ref[:, lo:hi] = (alpha * acc_ref[:, lo:hi]
                             + jnp.dot(vt, p, preferred_element_type=F32))
        m_ref[0:1, lo:hi] = m_new

    assert nsub == 2
    scores(0, 0)

    def block_pair(base):
        scores(base + 1, 1)
        update(base, 0)
        scores(base + 2, 0)
        update(base + 1, 1)

    def four_blocks(i, carry):
        block_pair(4 * i)
        block_pair(4 * i + 2)
        return carry

    lax.fori_loop(0, qi // 2, four_blocks, 0)

    @pl.when(qi % 2 == 1)
    def _():
        block_pair(2 * qi - 2)
    scores(2 * qi + 1, 1, lo=tk)
    update(2 * qi, 0, 0, tk, diagonal=True)
    update(2 * qi, 0, tk, tq)
    update(2 * qi + 1, 1, tk, tq, diagonal=True)
    out_t = acc_ref[0:hd, :] / acc_ref[hd:hd + 1, :]
    o_ref[...] = out_t.T.astype(o_ref.dtype)


def fox_attention(q, k, vt, c, n_heads, tq=1024, tk=512):
    S = q.shape[0]
    hd = FOX_HEAD
    tq = min(tq, S)
    assert tq % tk == 0 and vt.shape[-1] == tk
    return pl.pallas_call(
        functools.partial(_fox_kernel, tq=tq, tk=tk),
        grid=(n_heads, S // tq),
        in_specs=[pl.BlockSpec((tq, hd), lambda h, i: (i, h)),
                  pl.BlockSpec((S, hd), lambda h, i: (0, h)),
                  pl.BlockSpec((None, S // tk, hd, tk), lambda h, i: (h, 0, 0, 0)),
                  pl.BlockSpec((S, V7X_LANES), lambda h, i: (0, 0))],
        out_specs=pl.BlockSpec((tq, hd), lambda h, i: (i, h)),
        out_shape=jax.ShapeDtypeStruct((S, n_heads * hd), BF16),
        scratch_shapes=[pltpu.VMEM((S, V7X_LANES), F32),
                        pltpu.VMEM((8, tq), F32),
                        pltpu.VMEM((hd + FOX_ONES_ROWS, tq), F32),
                        pltpu.VMEM((2, tk, tq), F32)],
        compiler_params=_cparams("parallel", "arbitrary"),
        name="fox_attention",
    )(q, k, vt, c)


def _pad_cols(w, n):
    return jnp.pad(w, ((0, 0), (0, n - w.shape[1])))


def _pad_rows(w, n):
    return jnp.pad(w, ((0, n - w.shape[0]), (0, 0)))


def kernel(x, mix_norm, ffn_norm, final_norm, rwkv_x_mix, rwkv_w_rkv, rwkv_w0, rwkv_w1, rwkv_w2, rwkv_a0, rwkv_a1, rwkv_a2, rwkv_v0, rwkv_v1, rwkv_v2, rwkv_g1, rwkv_g2, rwkv_k_k, rwkv_k_a, rwkv_r_k, rwkv_lnx_w, rwkv_lnx_b, rwkv_w_o, kv_norm, w_kvf, b_f, fox_w_q, fox_w_o, mlp_w_up, mlp_w_down):
    B, S, D = x.shape
    assert B == 1
    n_a = rwkv_x_mix.shape[0]
    n_b = fox_w_q.shape[0]
    assert n_a >= 1 and n_b >= 1
    n_fox_heads = D // FOX_HEAD
    lora_pad = V7X_LANES
    fox_tk = min(512, S)
    xs = x.reshape(S, D)
    bf = lambda w: w.astype(BF16)
    w_up_bf, w_down_bf = bf(mlp_w_up), bf(mlp_w_down)

    v_first = None
    for i in range(n_a):
        w1 = bf(_pad_cols(rwkv_w1[i], lora_pad))
        a1 = bf(_pad_cols(rwkv_a1[i], lora_pad))
        v1 = bf(_pad_cols(rwkv_v1[i - 1], lora_pad)) if i > 0 else None
        mixed, hw, ha, hg, *hv = rwkv_mix(xs, mix_norm[i:i + 1], rwkv_x_mix[i], w1, a1,
                                          bf(rwkv_g1[i]), v1)
        rkv = matmul(mixed, rwkv_w_rkv, w_idx=(i,), name="rkv_proj")
        w2 = bf(_pad_rows(rwkv_w2[i], lora_pad))
        a2 = bf(_pad_rows(rwkv_a2[i], lora_pad))
        zeros = jnp.zeros((D,), F32)
        r_k = rwkv_r_k[i].reshape(D)
        if i == 0:
            vecs = jnp.stack([rwkv_w0[i], rwkv_a0[i], zeros, rwkv_k_k[i], rwkv_k_a[i], r_k,
                              zeros, zeros])
            pre = rwkv_pre(rkv, hw, ha, None, w2, a2, None, vecs, None)
            v_first = rkv
        else:
            vecs = jnp.stack([rwkv_w0[i], rwkv_a0[i], rwkv_v0[i - 1], rwkv_k_k[i], rwkv_k_a[i],
                              r_k, zeros, zeros])
            v2 = bf(_pad_rows(rwkv_v2[i - 1], lora_pad))
            pre = rwkv_pre(rkv, hw, ha, hv[0], w2, a2, v2, vecs, v_first)
        *rec_in, bonus = pre
        y = rwkv_recurrence(*rec_in)
        y = rwkv_post(y, bonus, hg, bf(rwkv_g2[i]), rwkv_lnx_w[i:i + 1], rwkv_lnx_b[i:i + 1])
        xs = matmul(y, rwkv_w_o, w_idx=(i,), res=xs, name="rwkv_out")
        if i + 1 < n_a:
            xs = mlp_block(xs, ffn_norm[i:i + 1], w_up_bf, w_down_bf, i)
        else:
            xs, normed = mlp_block(xs, ffn_norm[i:i + 1], w_up_bf, w_down_bf, i,
                                   post_gains=jnp.stack([mix_norm[n_a], kv_norm]))

    k_sh = vt = c = out = None
    for j in range(n_b):
        layer = n_a + j
        if j == 0:
            k_sh = matmul(normed, w_kvf, x_idx=1, n_out=D, out_dtype=BF16, name="k_proj")
            vt = matmul_head_transposed(normed, 1, w_kvf, D, n_fox_heads, FOX_HEAD, fox_tk)
            b_pad = jnp.pad(b_f, (0, V7X_LANES - n_fox_heads)).reshape(1, V7X_LANES)
            log_f = matmul(normed, bf(_pad_cols(w_kvf[:, 2 * D:], V7X_LANES)), x_idx=1,
                           bias=b_pad, act=_log_sigmoid, name="forget_proj")
            c = cumsum_rows(log_f)
        q = matmul(normed, fox_w_q, w_idx=(j,), x_idx=0, scale=LOG2_E * FOX_HEAD ** -0.5,
                   out_dtype=BF16, name="q_proj")
        o = fox_attention(q, k_sh, vt, c, n_fox_heads, tk=fox_tk)
        xs = matmul(o, fox_w_o, w_idx=(j,), res=xs, name="fox_out")
        if j + 1 < n_b:
            xs, normed = mlp_block(xs, ffn_norm[layer:layer + 1], w_up_bf, w_down_bf, layer,
                                   post_gains=mix_norm[layer + 1:layer + 2])
        else:
            out = mlp_block(xs, ffn_norm[layer:layer + 1], w_up_bf, w_down_bf, layer,
                            post_gains=final_norm.reshape(1, D), post_dtype=F32, emit_x=False)
    return out.reshape(B, S, D)
```

```python
import functools
import math

import jax
import jax.numpy as jnp
from jax import lax
from jax.experimental import pallas as pl
from jax.experimental.pallas import tpu as pltpu

F32 = jnp.float32
BF16 = jnp.bfloat16

RWKV_HEAD = 64
FOX_HEAD = 128
RMS_EPS = 1e-6
LNX_EPS = 1e-5 * RWKV_HEAD
V7X_LANES = 128
REC_CHUNK = 64
V7X_VMEM_LIMIT = 56 * 1024 * 1024
MASK_VALUE = -1e30
LOG2_E = math.log2(math.e)
DECAY_SCALE = math.exp(-0.5)


def _cparams(*sem):
    return pltpu.CompilerParams(dimension_semantics=sem, vmem_limit_bytes=V7X_VMEM_LIMIT)


def _sigmoid(z):
    return 1.0 / (1.0 + jnp.exp(-z))


def _log_sigmoid(z):
    return jnp.minimum(z, 0.0) - jnp.log(1.0 + jnp.exp(-jnp.abs(z)))


def _rms(x, gain):
    return x * lax.rsqrt(jnp.mean(x * x, axis=-1, keepdims=True) + RMS_EPS) * gain


def _head_sum_matrix(n, head):
    r = lax.broadcasted_iota(jnp.int32, (n, n), 0) // head
    c = lax.broadcasted_iota(jnp.int32, (n, n), 1) // head
    return (r == c).astype(BF16)


def _group_sum(x, ones_bd):
    return jnp.dot(x.astype(BF16), ones_bd, preferred_element_type=F32)


def _tri_sum(tri, x):
    hi = x.astype(BF16)
    r1 = x - hi.astype(F32)
    mid = r1.astype(BF16)
    lo = (r1 - mid.astype(F32)).astype(BF16)
    return (jnp.dot(tri, hi, preferred_element_type=F32)
            + jnp.dot(tri, mid, preferred_element_type=F32)
            + jnp.dot(tri, lo, preferred_element_type=F32))


def _mix_kernel(x_ref, halo_ref, g_ref, mix_ref, w1_ref, a1_ref, g1_ref, *rest, has_vres):
    if has_vres:
        v1_ref, o_ref, hw_ref, ha_ref, hg_ref, hv_ref = rest
    else:
        o_ref, hw_ref, ha_ref, hg_ref = rest
    i = pl.program_id(0)
    gain = g_ref[...]
    h = _rms(x_ref[...], gain)
    hh = _rms(halo_ref[...], gain)
    last = jnp.where(i == 0, 0.0, hh[7:8, :])
    row = lax.broadcasted_iota(jnp.int32, h.shape, 0)
    prev = jnp.where(row == 0, last, pltpu.roll(h, 1, 0))
    d = prev - h
    mixed = lambda n: (h + d * mix_ref[n:n + 1, :]).astype(BF16)
    lora = lambda n, w_ref: jnp.dot(mixed(n), w_ref[...], preferred_element_type=F32)
    xv = mixed(2)
    o_ref[0] = mixed(0)
    o_ref[1] = mixed(1)
    o_ref[2] = xv
    hw_ref[...] = jnp.tanh(lora(3, w1_ref)).astype(BF16)
    ha_ref[...] = lora(4, a1_ref).astype(BF16)
    hg_ref[...] = _sigmoid(lora(5, g1_ref)).astype(BF16)
    if has_vres:
        hv_ref[...] = jnp.dot(xv, v1_ref[...], preferred_element_type=F32).astype(BF16)


def rwkv_mix(x, gain, x_mix, w1, a1, g1, v1, tm=256):
    S, D = x.shape
    tm = min(tm, S)
    hb = tm // 8
    has_vres = v1 is not None
    row = lambda i: (i, 0)
    full = lambda i: (0, 0)
    loras = [w1, a1, g1] + ([v1] if has_vres else [])
    return pl.pallas_call(
        functools.partial(_mix_kernel, has_vres=has_vres),
        grid=(S // tm,),
        in_specs=[pl.BlockSpec((tm, D), row),
                  pl.BlockSpec((8, D), lambda i: (jnp.maximum(i * hb - 1, 0), 0)),
                  pl.BlockSpec((1, D), full),
                  pl.BlockSpec(x_mix.shape, full)]
                 + [pl.BlockSpec(w.shape, full) for w in loras],
        out_specs=[pl.BlockSpec((3, tm, D), lambda i: (0, i, 0))]
                  + [pl.BlockSpec((tm, w.shape[1]), row) for w in loras],
        out_shape=[jax.ShapeDtypeStruct((3, S, D), BF16)]
                  + [jax.ShapeDtypeStruct((S, w.shape[1]), BF16) for w in loras],
        compiler_params=_cparams("parallel"),
        name="rwkv_mix",
    )(x, x, gain, x_mix, *loras)


def _mm_kernel(x_ref, w_ref, *rest, act, scale, has_bias, has_res, cast_w, row_axis):
    if cast_w:
        o_ref, wbf_ref = rest[-2], rest[-1]

        @pl.when(pl.program_id(row_axis) == 0)
        def _():
            wbf_ref[...] = w_ref[...].astype(BF16)
        w = wbf_ref[...]
    else:
        o_ref = rest[-1]
        w = w_ref[...]
    acc = jnp.dot(x_ref[...], w, preferred_element_type=F32)
    if scale != 1.0:
        acc = acc * scale
    if has_bias:
        acc = acc + rest[0][...]
    if act is not None:
        acc = act(acc)
    if has_res:
        acc = acc + rest[1 if has_bias else 0][...]
    o_ref[...] = acc.astype(o_ref.dtype)


def matmul(x, w, *, x_idx=None, w_idx=(), n_out=None, act=None, scale=1.0, bias=None, res=None,
           out_dtype=F32, tm=1024, tn=1024, name="matmul"):
    M, K = x.shape[-2:]
    N = w.shape[-1] if n_out is None else n_out
    batched = w.ndim - len(w_idx) == 3
    tm, tn = min(tm, M), min(tn, N)
    cast_w = w.dtype != BF16
    lead = (None,) * len(w_idx)
    if batched:
        nb = w.shape[len(w_idx)]
        grid = (nb, N // tn, M // tm)
        x_spec = pl.BlockSpec((None, tm, K), lambda b, j, i: (b, i, 0))
        w_spec = pl.BlockSpec(lead + (None, K, tn), lambda b, j, i: w_idx + (b, 0, j))
        o_spec = pl.BlockSpec((None, tm, tn), lambda b, j, i: (b, i, j))
        out_shape = jax.ShapeDtypeStruct((nb, M, N), out_dtype)
        sem = ("parallel", "parallel", "arbitrary")
    else:
        grid = (N // tn, M // tm)
        if x_idx is None:
            x_spec = pl.BlockSpec((tm, K), lambda j, i: (i, 0))
        else:
            x_spec = pl.BlockSpec((None, tm, K), lambda j, i: (x_idx, i, 0))
        w_spec = pl.BlockSpec(lead + (K, tn), lambda j, i: w_idx + (0, j))
        o_spec = pl.BlockSpec((tm, tn), lambda j, i: (i, j))
        out_shape = jax.ShapeDtypeStruct((M, N), out_dtype)
        sem = ("parallel", "arbitrary")
    kern = functools.partial(_mm_kernel, act=act, scale=scale, has_bias=bias is not None,
                             has_res=res is not None, cast_w=cast_w, row_axis=len(grid) - 1)
    in_specs = [x_spec, w_spec]
    args = [x, w]
    if bias is not None:
        assert not batched
        in_specs.append(pl.BlockSpec((1, tn), lambda j, i: (0, j)))
        args.append(bias)
    if res is not None:
        assert not batched
        in_specs.append(o_spec)
        args.append(res)
    return pl.pallas_call(
        kern, grid=grid, in_specs=in_specs, out_specs=o_spec, out_shape=out_shape,
        scratch_shapes=[pltpu.VMEM((K, tn), BF16)] if cast_w else [],
        compiler_params=_cparams(*sem), name=name,
    )(*args)


def _proj_t_kernel(x_ref, w_ref, o_ref, wbf_ref):
    @pl.when(pl.program_id(1) == 0)
    def _():
        wbf_ref[...] = w_ref[...].astype(BF16)

    acc = jnp.dot(x_ref[...], wbf_ref[...], preferred_element_type=F32)
    n_heads, n_blocks, hd, tk = o_ref.shape
    for h in range(n_heads):
        for b in range(n_blocks):
            tile = acc[b * tk:(b + 1) * tk, h * hd:(h + 1) * hd]
            o_ref[h, b] = tile.T.astype(o_ref.dtype)


def matmul_head_transposed(x, x_idx, w, col0, n_heads, hd, tk, tm=1024, tn=1024):
    _, M, K = x.shape
    tm = min(tm, M)
    assert col0 % tn == 0 and (n_heads * hd) % tn == 0 and tn % hd == 0 and tm % tk == 0
    return pl.pallas_call(
        _proj_t_kernel,
        grid=(n_heads * hd // tn, M // tm),
        in_specs=[pl.BlockSpec((None, tm, K), lambda j, i: (x_idx, i, 0)),
                  pl.BlockSpec((K, tn), lambda j, i: (0, col0 // tn + j))],
        out_specs=pl.BlockSpec((tn // hd, tm // tk, hd, tk), lambda j, i: (j, i, 0, 0)),
        out_shape=jax.ShapeDtypeStruct((n_heads, M // tk, hd, tk), BF16),
        scratch_shapes=[pltpu.VMEM((K, tn), BF16)],
        compiler_params=_cparams("parallel", "arbitrary"),
        name="proj_head_transposed",
    )(x, w)


def _pre_kernel(*refs, has_vres):
    if has_vres:
        (r_ref, k_ref, v_ref, hw_ref, ha_ref, hv_ref, w2_ref, a2_ref, v2_ref,
         vec_ref, vf_ref, *outs) = refs
    else:
        (r_ref, k_ref, v_ref, hw_ref, ha_ref, w2_ref, a2_ref, vec_ref, *outs) = refs
    ao_ref, ro_ref, bo_ref, ko_ref, bdo_ref, kdo_ref, vo_ref, pend_ref, bonus_ref = outs
    tm, D = k_ref.shape
    T, W = REC_CHUNK, V7X_LANES
    ri = lax.broadcasted_iota(jnp.int32, (tm, tm), 0)
    ci = lax.broadcasted_iota(jnp.int32, (tm, tm), 1)
    tri = jnp.logical_and(ci <= ri, ci // T == ri // T).astype(BF16)
    row = lax.broadcasted_iota(jnp.int32, (tm, D), 0)
    ones_bd = _head_sum_matrix(W, RWKV_HEAD)
    w0, a0, v0, k_k, k_a, r_k = (vec_ref[n:n + 1, :] for n in range(6))

    def head_sum(x):
        return jnp.concatenate([_group_sum(x[:, s * W:(s + 1) * W], ones_bd)
                                for s in range(D // W)], axis=1)

    r, k, v = r_ref[...], k_ref[...], v_ref[...]
    z_w = w0 + jnp.dot(hw_ref[...], w2_ref[...], preferred_element_type=F32)
    lw = -DECAY_SCALE * _sigmoid(z_w)
    a_sig = _sigmoid(a0 + jnp.dot(ha_ref[...], a2_ref[...], preferred_element_type=F32))
    if has_vres:
        mix = _sigmoid(v0 + jnp.dot(hv_ref[...], v2_ref[...], preferred_element_type=F32))
        v = v + (vf_ref[...] - v) * mix
    vo_ref[...] = v.astype(BF16)
    kmod = k * (1.0 + (a_sig - 1.0) * k_a)
    bonus_ref[...] = head_sum(r * kmod * r_k) * v
    kk = k * k_k
    kk = kk * lax.rsqrt(jnp.maximum(head_sum(kk * kk), 1e-24))
    b = kk * a_sig

    L = _tri_sum(tri, lw)
    l_end = L[T - 1:T, :]
    for c in range(1, tm // T):
        l_end = jnp.where(row < c * T, l_end, L[(c + 1) * T - 1:(c + 1) * T, :])
    for c in range(tm // T):
        pend_ref[c] = jnp.broadcast_to(jnp.exp(L[(c + 1) * T - 1:(c + 1) * T, :]), (8, D))
    e_pos = jnp.exp(L)
    e_neg = 1.0 / e_pos
    to_end = jnp.exp(l_end - L)
    ao_ref[...] = (-kk * jnp.exp(L - lw)).astype(BF16)
    ro_ref[...] = (r * e_pos).astype(BF16)
    bo_ref[...] = (b * e_neg).astype(BF16)
    ko_ref[...] = (kmod * e_neg).astype(BF16)
    bdo_ref[...] = (b * to_end).astype(BF16)
    kdo_ref[...] = (kmod * to_end).astype(BF16)


def rwkv_pre(rkv, hw, ha, hv, w2, a2, v2, vecs, v_first, tm=256):
    _, S, D = rkv.shape
    tm = min(tm, S)
    assert tm % REC_CHUNK == 0
    has_vres = v_first is not None
    row = lambda i: (i, 0)
    full = lambda i: (0, 0)
    act = pl.BlockSpec((tm, D), row)
    rkv_spec = lambda n: pl.BlockSpec((None, tm, D), lambda i: (n, i, 0))
    lora = lambda a: pl.BlockSpec((tm, a.shape[1]), row)
    wspec = lambda a: pl.BlockSpec(a.shape, full)
    if has_vres:
        in_specs = [rkv_spec(0), rkv_spec(1), rkv_spec(2), lora(hw), lora(ha), lora(hv),
                    wspec(w2), wspec(a2), wspec(v2), wspec(vecs), rkv_spec(2)]
        args = [rkv, rkv, rkv, hw, ha, hv, w2, a2, v2, vecs, v_first]
    else:
        in_specs = [rkv_spec(0), rkv_spec(1), rkv_spec(2), lora(hw), lora(ha),
                    wspec(w2), wspec(a2), wspec(vecs)]
        args = [rkv, rkv, rkv, hw, ha, w2, a2, vecs]
    n_chunks = S // REC_CHUNK
    pend_spec = pl.BlockSpec((tm // REC_CHUNK, 8, D), lambda i: (i, 0, 0))
    bf_out = jax.ShapeDtypeStruct((S, D), BF16)
    f32_out = jax.ShapeDtypeStruct((S, D), F32)
    return pl.pallas_call(
        functools.partial(_pre_kernel, has_vres=has_vres),
        grid=(S // tm,),
        in_specs=in_specs,
        out_specs=[act] * 7 + [pend_spec, act],
        out_shape=[bf_out] * 7 + [jax.ShapeDtypeStruct((n_chunks, 8, D), F32), f32_out],
        compiler_params=_cparams("parallel"),
        name="rwkv_pre",
    )(*args)


def _rec_kernel(a_ref, r_ref, b_ref, k_ref, bd_ref, kd_ref, v_ref, pend_ref,
                bonus_ref, hg_ref, g2_ref, lnw_ref, lnb_ref, y_ref, st_ref):
    T, W = REC_CHUNK, V7X_LANES
    D = a_ref.shape[-1]
    slabs = range(D // W)

    @pl.when(pl.program_id(0) == 0)
    def _():
        st_ref[...] = jnp.zeros_like(st_ref)

    lane = lax.broadcasted_iota(jnp.int32, (T, W), 1)
    head0 = lane < RWKV_HEAD
    t_row = lax.broadcasted_iota(jnp.int32, (T, W), 0)
    strict = (lane & (T - 1)) < t_row
    incl = (lane & (T - 1)) <= t_row
    same_head = (lax.broadcasted_iota(jnp.int32, (W, W), 0) // RWKV_HEAD
                 == lax.broadcasted_iota(jnp.int32, (W, W), 1) // RWKV_HEAD)

    def stacked(x):
        x = x.astype(F32)
        return jnp.concatenate([jnp.where(head0, x, 0.0), jnp.where(head0, 0.0, x)],
                               axis=0).astype(BF16)

    def tile(ref, p):
        return ref[:, p * W:(p + 1) * W]

    def mm(x, y):
        return jnp.dot(x.astype(BF16), y.astype(BF16), preferred_element_type=F32)

    def mm_nt(x, y):
        return lax.dot_general(x.astype(BF16), y.astype(BF16), (((1,), (1,)), ((), ())),
                               preferred_element_type=F32)

    def mm_tn(x, y):
        return lax.dot_general(x.astype(BF16), y.astype(BF16), (((0,), (0,)), ((), ())),
                               preferred_element_type=F32)

    ar = [jnp.concatenate([tile(a_ref, p), tile(r_ref, p)], axis=0) for p in slabs]
    gram = [mm_nt(ar[p], jnp.concatenate([stacked(tile(b_ref, p)), stacked(tile(k_ref, p))],
                                         axis=0)) for p in slabs]
    n_w = [jnp.where(strict, g[:T, :W], 0.0) for g in gram]
    a_rb = [jnp.where(incl, g[T:, :W], 0.0).astype(BF16) for g in gram]
    av = [mm(jnp.concatenate([jnp.where(strict, g[:T, W:], 0.0),
                              jnp.where(incl, g[T:, W:], 0.0)], axis=0),
             stacked(tile(v_ref, p))) for p, g in zip(slabs, gram)]

    s0 = [st_ref[p] for p in slabs]
    wr = [mm_nt(ar[p], s0[p]) for p in slabs]

    u_w = [wr[p][:T] + av[p][:T] for p in slabs]
    for _ in range(5):
        z = [mm(n_w[p], jnp.concatenate([stacked(n_w[p]), stacked(u_w[p])], axis=1))
             for p in slabs]
        n_w = [zp[:, :W] for zp in z]
        u_w = [u_w[p] + z[p][:, W:] for p in slabs]
    u_w = [u_w[p] + mm(n_w[p], stacked(u_w[p])) for p in slabs]
    for p in slabs:
        upd = mm_tn(jnp.concatenate([u_w[p].astype(BF16), tile(v_ref, p)], axis=0),
                    jnp.concatenate([tile(bd_ref, p), tile(kd_ref, p)], axis=0))
        st_ref[p] = s0[p] * tile(pend_ref, p)[0:1, :] + jnp.where(same_head, upd, 0.0)
    ones_bd = _head_sum_matrix(W, RWKV_HEAD)
    inv_n = 1.0 / RWKV_HEAD
    hg = hg_ref[...]
    y = [wr[p][T:] + mm(a_rb[p], stacked(u_w[p])) + av[p][T:] for p in slabs]
    d = [y[p] - _group_sum(y[p], ones_bd) * inv_n for p in slabs]
    var = [_group_sum(d[p] * d[p], ones_bd) * inv_n for p in slabs]
    gate = [jnp.dot(hg, tile(g2_ref, p), preferred_element_type=F32) for p in slabs]
    for p in slabs:
        yn = d[p] * lax.rsqrt(var[p] + LNX_EPS) * tile(lnw_ref, p) + tile(lnb_ref, p)
        y_ref[:, p * W:(p + 1) * W] = ((yn + tile(bonus_ref, p)) * gate[p]).astype(y_ref.dtype)


def rwkv_recurrence(a, r, b, k, bd, kd, v, pend, bonus, hg, g2, lnw, lnb):
    S, D = a.shape
    T = REC_CHUNK
    act = pl.BlockSpec((T, D), lambda c: (c, 0))
    vec = pl.BlockSpec((1, D), lambda c: (0, 0))
    return pl.pallas_call(
        _rec_kernel,
        grid=(S // T,),
        in_specs=[act] * 7 + [pl.BlockSpec((None, 8, D), lambda c: (c, 0, 0)), act,
                              pl.BlockSpec((T, hg.shape[1]), lambda c: (c, 0)),
                              pl.BlockSpec(g2.shape, lambda c: (0, 0)), vec, vec],
        out_specs=act,
        out_shape=jax.ShapeDtypeStruct((S, D), BF16),
        scratch_shapes=[pltpu.VMEM((D // V7X_LANES, V7X_LANES, V7X_LANES), F32)],
        compiler_params=_cparams("arbitrary"),
        name="rwkv_recurrence",
    )(a, r, b, k, bd, kd, v, pend, bonus, hg, g2, lnw, lnb)


def _mlp_kernel(x_ref, g_ref, wu_ref, wd_ref, *rest, emit_x, has_post):
    if not has_post:
        acc_ref, xn_ref = rest
    elif emit_x:
        pg_ref, acc_ref, post_ref, xn_ref = rest
    else:
        pg_ref, post_ref, xn_ref, acc_ref = rest
    f = pl.program_id(1)

    @pl.when(f == 0)
    def _():
        x = x_ref[...]
        xn_ref[...] = _rms(x, g_ref[...]).astype(BF16)
        acc_ref[...] = x

    h = jnp.dot(xn_ref[...], wu_ref[...], preferred_element_type=F32)
    h = jnp.square(jnp.maximum(h, 0.0)).astype(BF16)
    acc_ref[...] += jnp.dot(h, wd_ref[...], preferred_element_type=F32)

    if has_post:
        @pl.when(f == pl.num_programs(1) - 1)
        def _():
            y = acc_ref[...]
            yn = y * lax.rsqrt(jnp.mean(y * y, axis=-1, keepdims=True) + RMS_EPS)
            for n in range(post_ref.shape[0]):
                post_ref[n] = (yn * pg_ref[n:n + 1, :]).astype(post_ref.dtype)


def mlp_block(x, gain, w_up, w_down, layer, post_gains=None, post_dtype=BF16, emit_x=True,
              tm=512, tf=1024):
    S, D = x.shape
    F = w_up.shape[-1]
    has_post = post_gains is not None
    assert has_post or emit_x
    tm, tf = min(tm, S), min(tf, F)
    row = pl.BlockSpec((tm, D), lambda i, f: (i, 0))
    in_specs = [row,
                pl.BlockSpec((1, D), lambda i, f: (0, 0)),
                pl.BlockSpec((None, D, tf), lambda i, f: (layer, 0, f)),
                pl.BlockSpec((None, tf, D), lambda i, f: (layer, f, 0))]
    args = [x, gain, w_up, w_down]
    out_specs = [row] if emit_x else []
    out_shape = [jax.ShapeDtypeStruct((S, D), F32)] if emit_x else []
    if has_post:
        G = post_gains.shape[0]
        in_specs.append(pl.BlockSpec((G, D), lambda i, f: (0, 0)))
        args.append(post_gains)
        out_specs.append(pl.BlockSpec((G, tm, D), lambda i, f: (0, i, 0)))
        out_shape.append(jax.ShapeDtypeStruct((G, S, D), post_dtype))
    outs = pl.pallas_call(
        functools.partial(_mlp_kernel, emit_x=emit_x, has_post=has_post),
        grid=(S // tm, F // tf),
        in_specs=in_specs,
        out_specs=out_specs,
        out_shape=out_shape,
        scratch_shapes=[pltpu.VMEM((tm, D), BF16)] + ([] if emit_x else [pltpu.VMEM((tm, D), F32)]),
        compiler_params=_cparams("parallel", "arbitrary"),
        name="mlp",
    )(*args)
    return outs if len(outs) > 1 else outs[0]


def _cumsum_kernel(x_ref, o_ref, carry_ref):
    @pl.when(pl.program_id(0) == 0)
    def _():
        carry_ref[...] = jnp.zeros_like(carry_ref)

    n = x_ref.shape[0]
    tri = (lax.broadcasted_iota(jnp.int32, (n, n), 1)
           <= lax.broadcasted_iota(jnp.int32, (n, n), 0)).astype(BF16)
    c = _tri_sum(tri, x_ref[...]) + carry_ref[0:1, :]
    o_ref[...] = c
    carry_ref[...] = jnp.broadcast_to(c[n - 1:n, :], carry_ref.shape)


def cumsum_rows(x, tc=256):
    S, W = x.shape
    tc = min(tc, S)
    return pl.pallas_call(
        _cumsum_kernel,
        grid=(S // tc,),
        in_specs=[pl.BlockSpec((tc, W), lambda i: (i, 0))],
        out_specs=pl.BlockSpec((tc, W), lambda i: (i, 0)),
        out_shape=jax.ShapeDtypeStruct((S, W), F32),
        scratch_shapes=[pltpu.VMEM((8, W), F32)],
        compiler_params=_cparams("arbitrary"),
        name="forget_cumsum",
    )(x)


FOX_ONES_ROWS = 16


def _fox_kernel(q_ref, k_ref, vt_ref, c_ref, o_ref, ck_ref, m_ref, acc_ref, s_ref, *, tq, tk):
    h = pl.program_id(0)
    qi = pl.program_id(1)
    nsub = tq // tk
    hd = k_ref.shape[1]
    S = k_ref.shape[0]

    @pl.when(qi == 0)
    def _():
        lane = lax.broadcasted_iota(jnp.int32, (tk, V7X_LANES), 1)
        for j in range(S // tk):
            col = jnp.sum(jnp.where(lane == h, c_ref[j * tk:(j + 1) * tk, :], 0.0),
                          axis=1, keepdims=True)
            ck_ref[j * tk:(j + 1) * tk, :] = jnp.broadcast_to(col * LOG2_E, (tk, V7X_LANES))

    m_ref[...] = jnp.full_like(m_ref, MASK_VALUE)
    acc_ref[...] = jnp.zeros_like(acc_ref)
    q = q_ref[...]
    ones_rows = jnp.ones((FOX_ONES_ROWS, tk), BF16)

    def scores(kj, slot, lo=0):
        start = pl.multiple_of(kj * tk, tk)
        s_ref[slot, :, lo:] = lax.dot_general(
            k_ref[pl.ds(start, tk), :], q[lo:], (((1,), (1,)), ((), ())),
            preferred_element_type=F32)

    def update(kj, slot, lo=0, hi=tq, diagonal=False):
        start = pl.multiple_of(kj * tk, tk)
        ck = ck_ref[pl.ds(start, tk), :]
        s = s_ref[slot, :, lo:hi] - jnp.concatenate([ck] * ((hi - lo) // V7X_LANES), axis=1)
        if diagonal:
            key = lax.broadcasted_iota(jnp.int32, (tk, tk), 0)
            qry = lax.broadcasted_iota(jnp.int32, (tk, tk), 1)
            s = jnp.where(qry >= key, s, MASK_VALUE)
        m_prev = m_ref[0:1, lo:hi]
        m_new = jnp.maximum(m_prev, jnp.max(s, axis=0, keepdims=True))
        alpha = jnp.exp2(m_prev - m_new)
        p = jnp.exp2(s - m_new).astype(BF16)
        vt = jnp.concatenate([vt_ref[kj], ones_rows], axis=0)
        acc_ref[:, lo:hi] = (alpha * acc_ref[:, lo:hi]
                             + jnp.dot(vt, p, preferred_element_type=F32))
        m_ref[0:1, lo:hi] = m_new

    assert nsub == 2
    scores(0, 0)

    def block_pair(base):
        scores(base + 1, 1)
        update(base, 0)
        scores(base + 2, 0)
        update(base + 1, 1)

    def four_blocks(i, carry):
        block_pair(4 * i)
        block_pair(4 * i + 2)
        return carry

    lax.fori_loop(0, qi // 2, four_blocks, 0)

    @pl.when(qi % 2 == 1)
    def _():
        block_pair(2 * qi - 2)
    scores(2 * qi + 1, 1, lo=tk)
    update(2 * qi, 0, 0, tk, diagonal=True)
    update(2 * qi, 0, tk, tq)
    update(2 * qi + 1, 1, tk, tq, diagonal=True)
    out_t = acc_ref[0:hd, :] / acc_ref[hd:hd + 1, :]
    o_ref[...] = out_t.T.astype(o_ref.dtype)


def fox_attention(q, k, vt, c, n_heads, tq=1024, tk=512):
    S = q.shape[0]
    hd = FOX_HEAD
    tq = min(tq, S)
    assert tq % tk == 0 and vt.shape[-1] == tk
    return pl.pallas_call(
        functools.partial(_fox_kernel, tq=tq, tk=tk),
        grid=(n_heads, S // tq),
        in_specs=[pl.BlockSpec((tq, hd), lambda h, i: (i, h)),
                  pl.BlockSpec((S, hd), lambda h, i: (0, h)),
                  pl.BlockSpec((None, S // tk, hd, tk), lambda h, i: (h, 0, 0, 0)),
                  pl.BlockSpec((S, V7X_LANES), lambda h, i: (0, 0))],
        out_specs=pl.BlockSpec((tq, hd), lambda h, i: (i, h)),
        out_shape=jax.ShapeDtypeStruct((S, n_heads * hd), BF16),
        scratch_shapes=[pltpu.VMEM((S, V7X_LANES), F32),
                        pltpu.VMEM((8, tq), F32),
                        pltpu.VMEM((hd + FOX_ONES_ROWS, tq), F32),
                        pltpu.VMEM((2, tk, tq), F32)],
        compiler_params=_cparams("parallel", "arbitrary"),
        name="fox_attention",
    )(q, k, vt, c)


def _pad_cols(w, n):
    return jnp.pad(w, ((0, 0), (0, n - w.shape[1])))


def _pad_rows(w, n):
    return jnp.pad(w, ((0, n - w.shape[0]), (0, 0)))


def kernel(x, mix_norm, ffn_norm, final_norm, rwkv_x_mix, rwkv_w_rkv, rwkv_w0, rwkv_w1, rwkv_w2, rwkv_a0, rwkv_a1, rwkv_a2, rwkv_v0, rwkv_v1, rwkv_v2, rwkv_g1, rwkv_g2, rwkv_k_k, rwkv_k_a, rwkv_r_k, rwkv_lnx_w, rwkv_lnx_b, rwkv_w_o, kv_norm, w_kvf, b_f, fox_w_q, fox_w_o, mlp_w_up, mlp_w_down):
    B, S, D = x.shape
    assert B == 1
    n_a = rwkv_x_mix.shape[0]
    n_b = fox_w_q.shape[0]
    assert n_a >= 1 and n_b >= 1
    n_fox_heads = D // FOX_HEAD
    lora_pad = V7X_LANES
    fox_tk = min(512, S)
    xs = x.reshape(S, D)
    bf = lambda w: w.astype(BF16)
    w_up_bf, w_down_bf = bf(mlp_w_up), bf(mlp_w_down)

    v_first = None
    for i in range(n_a):
        w1 = bf(_pad_cols(rwkv_w1[i], lora_pad))
        a1 = bf(_pad_cols(rwkv_a1[i], lora_pad))
        v1 = bf(_pad_cols(rwkv_v1[i - 1], lora_pad)) if i > 0 else None
        mixed, hw, ha, hg, *hv = rwkv_mix(xs, mix_norm[i:i + 1], rwkv_x_mix[i], w1, a1,
                                          bf(rwkv_g1[i]), v1)
        rkv = matmul(mixed, rwkv_w_rkv, w_idx=(i,), name="rkv_proj")
        w2 = bf(_pad_rows(rwkv_w2[i], lora_pad))
        a2 = bf(_pad_rows(rwkv_a2[i], lora_pad))
        zeros = jnp.zeros((D,), F32)
        r_k = rwkv_r_k[i].reshape(D)
        if i == 0:
            vecs = jnp.stack([rwkv_w0[i], rwkv_a0[i], zeros, rwkv_k_k[i], rwkv_k_a[i], r_k,
                              zeros, zeros])
            pre = rwkv_pre(rkv, hw, ha, None, w2, a2, None, vecs, None)
            v_first = rkv
        else:
            vecs = jnp.stack([rwkv_w0[i], rwkv_a0[i], rwkv_v0[i - 1], rwkv_k_k[i], rwkv_k_a[i],
                              r_k, zeros, zeros])
            v2 = bf(_pad_rows(rwkv_v2[i - 1], lora_pad))
            pre = rwkv_pre(rkv, hw, ha, hv[0], w2, a2, v2, vecs, v_first)
        *rec_in, bonus = pre
        y = rwkv_recurrence(*rec_in, bonus, hg, bf(rwkv_g2[i]), rwkv_lnx_w[i:i + 1],
                            rwkv_lnx_b[i:i + 1])
        xs = matmul(y, rwkv_w_o, w_idx=(i,), res=xs, name="rwkv_out")
        if i + 1 < n_a:
            xs = mlp_block(xs, ffn_norm[i:i + 1], w_up_bf, w_down_bf, i)
        else:
            xs, normed = mlp_block(xs, ffn_norm[i:i + 1], w_up_bf, w_down_bf, i,
                                   post_gains=jnp.stack([mix_norm[n_a], kv_norm]))

    k_sh = vt = c = out = None
    for j in range(n_b):
        layer = n_a + j
        if j == 0:
            k_sh = matmul(normed, w_kvf, x_idx=1, n_out=D, out_dtype=BF16, name="k_proj")
            vt = matmul_head_transposed(normed, 1, w_kvf, D, n_fox_heads, FOX_HEAD, fox_tk)
            b_pad = jnp.pad(b_f, (0, V7X_LANES - n_fox_heads)).reshape(1, V7X_LANES)
            log_f = matmul(normed, bf(_pad_cols(w_kvf[:, 2 * D:], V7X_LANES)), x_idx=1,
                           bias=b_pad, act=_log_sigmoid, name="forget_proj")
            c = cumsum_rows(log_f)
        q = matmul(normed, fox_w_q, w_idx=(j,), x_idx=0, scale=LOG2_E * FOX_HEAD ** -0.5,
                   out_dtype=BF16, name="q_proj")
        o = fox_attention(q, k_sh, vt, c, n_fox_heads, tk=fox_tk)
        xs = matmul(o, fox_w_o, w_idx=(j,), res=xs, name="fox_out")
        if j + 1 < n_b:
            xs, normed = mlp_block(xs, ffn_norm[layer:layer + 1], w_up_bf, w_down_bf, layer,
                                   post_gains=mix_norm[layer + 1:layer + 2])
        else:
            out = mlp_block(xs, ffn_norm[layer:layer + 1], w_up_bf, w_down_bf, layer,
                            post_gains=final_norm.reshape(1, D), post_dtype=F32, emit_x=False)
    return out.reshape(B, S, D)
```
